```python
import math
import jax, jax.numpy as jnp
from jax import lax
import numpy as np

D_MODEL = 2048
BATCH = 4
SEQ = 8192
DEPTH = 2
DEC_BATCH = 16
DEC_SEQ = 32
PAST_LEN = 4096

CHUNK = 64
Q_BLOCK = 128
HEAD_DIM = 128
A_HEADS = 6
B_HEADS = 6
C_HEADS = 4
C_QK_DIM = 64
C_V_DIM = 2 * C_QK_DIM
A_WIDTH = A_HEADS * HEAD_DIM
B_WIDTH = B_HEADS * HEAD_DIM
C_QK_WIDTH = C_HEADS * 2 * C_QK_DIM
C_WIDTH = C_HEADS * C_V_DIM
MIX_WIDTH = A_WIDTH + B_WIDTH + C_WIDTH
IDX_HEADS = 16
IDX_DIM = 128
IDX_TOPK = 256
ROPE_THETA = 500000.0
ROPE_FRACTION = 4
N_EXPERTS = 32
TOP_K = 4
D_FF = D_MODEL
SWIGLU_LIMIT = 7.0
SWIGLU_ALPHA = 1.702
RMS_EPS = 1e-6
SUBLN_EPS = 1e-5
NEG_INF = -1e30
IN_SPLITS = (A_WIDTH, A_WIDTH, A_WIDTH, A_HEADS,
             B_WIDTH, B_WIDTH, B_WIDTH, IDX_HEADS * IDX_DIM, IDX_DIM, IDX_HEADS,
             C_QK_WIDTH, C_QK_WIDTH, C_WIDTH)
N_IN = 3 * A_WIDTH + A_HEADS + 3 * B_WIDTH + IDX_HEADS * IDX_DIM + IDX_DIM + IDX_HEADS + 2 * C_QK_WIDTH + C_WIDTH

kernel_name = 'hybrid_streaming_fox_dsa_diff_moe_step'


def _rms_norm(x, g, eps=RMS_EPS):
    xf = x.astype(jnp.float32)
    y = xf * lax.rsqrt(jnp.mean(xf * xf, axis=-1, keepdims=True) + eps)
    return (y * g.astype(jnp.float32)).astype(x.dtype)


def _rope(x, pos):
    d = x.shape[-1]
    rot = d // ROPE_FRACTION
    half = rot // 2
    inv_freq = ROPE_THETA ** (-jnp.arange(half, dtype=jnp.float32) / half)
    ang = pos.astype(jnp.float32)[:, None] * inv_freq[None, :]
    cos = jnp.cos(ang)[:, None, :]
    sin = jnp.sin(ang)[:, None, :]
    xf = x[..., :rot].astype(jnp.float32)
    x1, x2 = xf[..., :half], xf[..., half:]
    xr = jnp.concatenate([x1 * cos - x2 * sin, x2 * cos + x1 * sin], axis=-1).astype(x.dtype)
    return jnp.concatenate([xr, x[..., rot:]], axis=-1)


def _sweep(block_fn, q_pos, *q_arrays):
    tq = q_pos.shape[0]
    if tq <= Q_BLOCK:
        return block_fn(q_pos, *q_arrays)
    nb = tq // Q_BLOCK
    pos_b = q_pos.reshape(nb, Q_BLOCK)
    arrs_b = tuple(jnp.moveaxis(a.reshape(a.shape[0], nb, Q_BLOCK, *a.shape[2:]), 1, 0) for a in q_arrays)
    out = lax.map(lambda args: block_fn(args[0], *args[1:]), (pos_b,) + arrs_b)
    out = jnp.moveaxis(out, 0, 1)
    return out.reshape(out.shape[0], tq, *out.shape[3:])


def _project(h, pos, w_in, b_forget):
    B, T, _ = h.shape
    z = jnp.einsum('btd,dn->btn', h, w_in)
    offs, acc = [], 0
    for s in IN_SPLITS[:-1]:
        acc += s
        offs.append(acc)
    fq, fk, fv, ff, dq, dk, dv, iq, ik, iw, cq, ck, cv = jnp.split(z, offs, axis=-1)
    fq = fq.reshape(B, T, A_HEADS, HEAD_DIM)
    fk = fk.reshape(B, T, A_HEADS, HEAD_DIM)
    fv = fv.reshape(B, T, A_HEADS, HEAD_DIM)
    logf = jax.nn.log_sigmoid(ff.astype(jnp.float32) + b_forget.astype(jnp.float32))
    dq = _rope(dq.reshape(B, T, B_HEADS, HEAD_DIM), pos)
    dk = _rope(dk.reshape(B, T, B_HEADS, HEAD_DIM), pos)
    dv = dv.reshape(B, T, B_HEADS, HEAD_DIM)
    iq = _rope(iq.reshape(B, T, IDX_HEADS, IDX_DIM), pos)
    ik = _rope(ik.reshape(B, T, 1, IDX_DIM), pos)[:, :, 0]
    iw = iw.astype(jnp.float32) * (IDX_HEADS ** -0.5 * IDX_DIM ** -0.5)
    cq = _rope(cq.reshape(B, T, C_HEADS * 2, C_QK_DIM), pos).reshape(B, T, C_HEADS, 2, C_QK_DIM)
    ck = _rope(ck.reshape(B, T, C_HEADS * 2, C_QK_DIM), pos).reshape(B, T, C_HEADS, 2, C_QK_DIM)
    cv = cv.reshape(B, T, C_HEADS, C_V_DIM)
    return (fq, fk, fv, logf, dq, dk, dv, iq, ik, iw, cq, ck, cv)


def _forgetting_attention(q, k, v, cum_q, cum_k, q_pos, k_pos):
    scale = q.shape[-1] ** -0.5
    cum_k_t = jnp.transpose(cum_k, (0, 2, 1))[:, :, None, :]

    def block(qp, qb, cqb):
        s = jnp.einsum('bqhd,bkhd->bhqk', qb, k).astype(jnp.float32) * scale
        s = s + jnp.transpose(cqb, (0, 2, 1))[..., None] - cum_k_t
        mask = k_pos[None, :] <= qp[:, None]
        p = jax.nn.softmax(jnp.where(mask, s, NEG_INF), axis=-1)
        return jnp.einsum('bhqk,bkhd->bqhd', p.astype(v.dtype), v)

    return _sweep(block, q_pos, q, cum_q)


def _dsa_attention(q, k, v, iq, ik, iw, q_pos, k_pos):
    n_keys = k.shape[1]
    top_k = min(IDX_TOPK, n_keys // 4)
    scale = q.shape[-1] ** -0.5
    gather = jax.vmap(lambda rows, idx: rows[idx])

    def block(qp, qb, iqb, iwb):
        limit = (qp // CHUNK + 1) * CHUNK
        rel = jax.nn.relu(jnp.einsum('bqhd,bkd->bqhk', iqb, ik).astype(jnp.float32))
        score = jnp.einsum('bqhk,bqh->bqk', rel, iwb)
        score = jnp.where(k_pos[None, None, :] < limit[None, :, None], score, NEG_INF)
        _, idx = lax.top_k(score, top_k)
        valid = k_pos[idx] < limit[None, :, None]
        kg = gather(k, idx)
        vg = gather(v, idx)
        s = jnp.einsum('bqhd,bqkhd->bhqk', qb, kg).astype(jnp.float32) * scale
        p = jax.nn.softmax(jnp.where(valid[:, None], s, NEG_INF), axis=-1)
        return jnp.einsum('bhqk,bqkhd->bqhd', p.astype(v.dtype), vg)

    return _sweep(block, q_pos, q, iq, iw)


def _diff_attention(q, k, v, q_pos, k_pos, lam, lam_init, g_sub):
    scale = q.shape[-1] ** -0.5

    def block(qp, qb):
        s = jnp.einsum('bqhmd,bkhmd->bhmqk', qb, k).astype(jnp.float32) * scale
        mask = k_pos[None, :] < ((qp // CHUNK + 1) * CHUNK)[:, None]
        p = jax.nn.softmax(jnp.where(mask, s, NEG_INF), axis=-1)
        a = p[:, :, 0] - lam * p[:, :, 1]
        o = jnp.einsum('bhqk,bkhd->bqhd', a, v.astype(jnp.float32))
        o = o * lax.rsqrt(jnp.mean(o * o, axis=-1, keepdims=True) + SUBLN_EPS)
        o = o * g_sub.astype(jnp.float32) * (1.0 - lam_init)
        return o.astype(v.dtype)

    return _sweep(block, q_pos, q)


def _mix(proj, past, q_pos, lam, lam_init, g_diff):
    fq, fk, fv, logf, dq, dk, dv, iq, ik, iw, cq, ck, cv = proj
    B, T = fq.shape[:2]
    rows = (fk, fv, logf, dk, dv, ik, ck, cv)
    if past is None:
        keys = rows
        k_pos = q_pos
    else:
        keys = tuple(jnp.concatenate([pr, r], axis=1) for pr, r in zip(past, rows))
        k_pos = jnp.arange(past[0].shape[1] + T, dtype=jnp.int32)
    fk_a, fv_a, logf_a, dk_a, dv_a, ik_a, ck_a, cv_a = keys
    cum = jnp.cumsum(logf_a.astype(jnp.float32), axis=1)
    o_a = _forgetting_attention(fq, fk_a, fv_a, cum[:, -T:], cum, q_pos, k_pos)
    o_b = _dsa_attention(dq, dk_a, dv_a, iq, ik_a, iw, q_pos, k_pos)
    o_c = _diff_attention(cq, ck_a, cv_a, q_pos, k_pos, lam, lam_init, g_diff)
    o = jnp.concatenate([o_a.reshape(B, T, -1), o_b.reshape(B, T, -1).astype(o_a.dtype),
                         o_c.reshape(B, T, -1).astype(o_a.dtype)], axis=-1)
    return o, rows


def _moe(h, w_router, b_router, w_gate_up, b_gate_up, w_down, b_down):
    n = h.shape[0]
    logits = jnp.einsum('nd,de->ne', h, w_router).astype(jnp.float32) + b_router.astype(jnp.float32)
    top_v, top_i = lax.top_k(logits, TOP_K)
    top_w = jax.nn.softmax(top_v, axis=-1)
    gates = jnp.einsum('nk,nke->en', top_w, jax.nn.one_hot(top_i, N_EXPERTS, dtype=jnp.float32))

    def expert(acc, xs):
        wgu, bgu, wd, bd, g = xs
        gu = jnp.einsum('nd,df->nf', h, wgu) + bgu
        gate = jnp.minimum(gu[:, :D_FF], SWIGLU_LIMIT)
        up = jnp.clip(gu[:, D_FF:], -SWIGLU_LIMIT, SWIGLU_LIMIT)
        act = (up + 1.0) * (gate * jax.nn.sigmoid(gate * SWIGLU_ALPHA))
        y = jnp.einsum('nf,fd->nd', act, wd) + bd
        return acc + g[:, None] * y.astype(jnp.float32), None

    acc, _ = lax.scan(expert, jnp.zeros((n, h.shape[1]), jnp.float32),
                      (w_gate_up, b_gate_up, w_down, b_down, gates))
    return acc.astype(h.dtype)


def _layer(x, c, q_pos, past, lam, lam_init, w_ada, b_ada, g_attn, g_ffn, w_in, b_forget, g_diff,
           w_out, w_router, b_router, w_gate_up, b_gate_up, w_down, b_down):
    B, T, D = x.shape
    mod = jnp.einsum('bd,dn->bn', jax.nn.silu(c), w_ada) + b_ada
    sh1, sc1, gt1, sh2, sc2, gt2 = [m[:, None, :] for m in jnp.split(mod, 6, axis=-1)]
    h = _rms_norm(x, g_attn) * (1.0 + sc1) + sh1
    proj = _project(h, q_pos, w_in, b_forget)
    o, rows = _mix(proj, past, q_pos, lam, lam_init, g_diff)
    x = x + gt1 * jnp.einsum('btm,md->btd', o, w_out)
    h = _rms_norm(x, g_ffn) * (1.0 + sc2) + sh2
    x = x + gt2 * _moe(h.reshape(B * T, D), w_router, b_router, w_gate_up, b_gate_up,
                       w_down, b_down).reshape(B, T, D)
    return x, rows


def _stack(rows, i):
    return jnp.stack([r[i] for r in rows], axis=0)


def setup_inputs(seed: int = 0) -> dict:
    key = jax.random.key(seed)
    ks = jax.random.split(key, 32)
    nrm = jax.random.normal
    f = jnp.float32
    P = PAST_LEN
    return {
        'x_prompt': nrm(ks[0], (BATCH, SEQ, D_MODEL), f),
        'x_sample': nrm(ks[1], (DEC_BATCH, DEC_SEQ, D_MODEL), f),
        'cache_fox_k': nrm(ks[2], (DEPTH, DEC_BATCH, P, A_HEADS, HEAD_DIM), f),
        'cache_fox_v': nrm(ks[3], (DEPTH, DEC_BATCH, P, A_HEADS, HEAD_DIM), f),
        'cache_fox_logf': jax.nn.log_sigmoid(2.0 + nrm(ks[4], (DEPTH, DEC_BATCH, P, A_HEADS), f)),
        'cache_dsa_k': nrm(ks[5], (DEPTH, DEC_BATCH, P, B_HEADS, HEAD_DIM), f),
        'cache_dsa_v': nrm(ks[6], (DEPTH, DEC_BATCH, P, B_HEADS, HEAD_DIM), f),
        'cache_dsa_idx_k': nrm(ks[7], (DEPTH, DEC_BATCH, P, IDX_DIM), f),
        'cache_diff_k': nrm(ks[8], (DEPTH, DEC_BATCH, P, C_HEADS, 2, C_QK_DIM), f),
        'cache_diff_v': nrm(ks[9], (DEPTH, DEC_BATCH, P, C_HEADS, C_V_DIM), f),
        'c_prompt': nrm(ks[10], (BATCH, D_MODEL), f),
        'c_sample': nrm(ks[11], (DEC_BATCH, D_MODEL), f),
        'w_ada': nrm(ks[12], (DEPTH, D_MODEL, 6 * D_MODEL), f) * (0.5 * D_MODEL ** -0.5),
        'b_ada': 0.02 * nrm(ks[13], (DEPTH, 6 * D_MODEL), f),
        'g_attn': 1.0 + 0.05 * nrm(ks[14], (DEPTH, D_MODEL), f),
        'g_ffn': 1.0 + 0.05 * nrm(ks[15], (DEPTH, D_MODEL), f),
        'w_in': nrm(ks[16], (DEPTH, D_MODEL, N_IN), f) * D_MODEL ** -0.5,
        'b_forget': 2.0 + 0.1 * nrm(ks[17], (DEPTH, A_HEADS), f),
        'diff_lambda': 0.1 * nrm(ks[18], (DEPTH, 4, C_QK_DIM), f),
        'g_diff': 1.0 + 0.05 * nrm(ks[19], (DEPTH, C_V_DIM), f),
        'w_out': nrm(ks[20], (DEPTH, MIX_WIDTH, D_MODEL), f) * MIX_WIDTH ** -0.5,
        'w_router': nrm(ks[21], (DEPTH, D_MODEL, N_EXPERTS), f) * D_MODEL ** -0.5,
        'b_router': 0.01 * nrm(ks[22], (DEPTH, N_EXPERTS), f),
        'w_gate_up': nrm(ks[23], (DEPTH, N_EXPERTS, D_MODEL, 2 * D_FF), f) * D_MODEL ** -0.5,
        'b_gate_up': 0.01 * nrm(ks[24], (DEPTH, N_EXPERTS, 2 * D_FF), f),
        'w_down': nrm(ks[25], (DEPTH, N_EXPERTS, D_FF, D_MODEL), f) * D_FF ** -0.5,
        'b_down': 0.01 * nrm(ks[26], (DEPTH, N_EXPERTS, D_MODEL), f),
        'g_final': 1.0 + 0.05 * nrm(ks[27], (D_MODEL,), f),
    }


def reference(x_prompt, x_sample, cache_fox_k, cache_fox_v, cache_fox_logf, cache_dsa_k, cache_dsa_v,
              cache_dsa_idx_k, cache_diff_k, cache_diff_v, c_prompt, c_sample, w_ada, b_ada, g_attn, g_ffn,
              w_in, b_forget, diff_lambda, g_diff, w_out, w_router, b_router, w_gate_up, b_gate_up,
              w_down, b_down, g_final):
    xp, xs = x_prompt, x_sample
    pos_p = jnp.arange(x_prompt.shape[1], dtype=jnp.int32)
    past_len = cache_fox_k.shape[2]
    pos_s = jnp.arange(past_len, past_len + x_sample.shape[1], dtype=jnp.int32)
    rows_p, rows_s = [], []
    for l in range(DEPTH):
        lam_init = 0.8 - 0.6 * math.exp(-0.3 * l)
        dl = diff_lambda[l].astype(jnp.float32)
        lam = jnp.exp(jnp.sum(dl[0] * dl[1])) - jnp.exp(jnp.sum(dl[2] * dl[3])) + lam_init
        weights = (w_ada[l], b_ada[l], g_attn[l], g_ffn[l], w_in[l], b_forget[l], g_diff[l], w_out[l],
                   w_router[l], b_router[l], w_gate_up[l], b_gate_up[l], w_down[l], b_down[l])
        past = (cache_fox_k[l], cache_fox_v[l], cache_fox_logf[l], cache_dsa_k[l], cache_dsa_v[l],
                cache_dsa_idx_k[l], cache_diff_k[l], cache_diff_v[l])
        xp, rp = _layer(xp, c_prompt, pos_p, None, lam, lam_init, *weights)
        xs, rs = _layer(xs, c_sample, pos_s, past, lam, lam_init, *weights)
        rows_p.append(rp)
        rows_s.append(rs)
    y_prompt = _rms_norm(xp, g_final)
    y_sample = _rms_norm(xs, g_final)
    return (y_prompt, y_sample,
            _stack(rows_p, 0), _stack(rows_p, 1), _stack(rows_p, 2), _stack(rows_p, 3),
            _stack(rows_p, 4), _stack(rows_p, 5), _stack(rows_p, 6), _stack(rows_p, 7),
            _stack(rows_s, 0), _stack(rows_s, 1), _stack(rows_s, 2), _stack(rows_s, 3),
            _stack(rows_s, 4), _stack(rows_s, 5), _stack(rows_s, 6), _stack(rows_s, 7))
```

```python
import functools
import math

import jax
import jax.numpy as jnp
from jax import lax
from jax.experimental import pallas as pl
from jax.experimental.pallas import tpu as pltpu

F32 = jnp.float32
BF16 = jnp.bfloat16
I32 = jnp.int32

LANES = 128
HEAD_DIM = 128
A_HEADS = 6
B_HEADS = 6
C_HEADS = 4
C_QK_DIM = 64
IDX_HEADS = 16
IDX_DIM = 128
IDX_TOPK = 256
CHUNK = 64
ROPE_THETA = 500000.0
ROPE_FRACTION = 4
TOP_K = 4
SWIGLU_LIMIT = 7.0
SWIGLU_ALPHA = 1.702
RMS_EPS = 1e-6
SUBLN_EPS = 1e-5
NEG_INF = -1e30
INT_MIN = -2 ** 31

A_WIDTH = A_HEADS * HEAD_DIM
B_WIDTH = B_HEADS * HEAD_DIM
C_WIDTH = C_HEADS * 2 * C_QK_DIM
IDX_WIDTH = IDX_HEADS * IDX_DIM
N_SMALL_FORGET = A_HEADS
N_SMALL_IW = IDX_HEADS

PROJ_TN = 256
VMEM_LIMIT = 56 * 1024 * 1024


def _cparams(sem):
    return pltpu.CompilerParams(dimension_semantics=sem, vmem_limit_bytes=VMEM_LIMIT)


def _dot_nt(a, b):
    return lax.dot_general(a, b, (((1,), (1,)), ((), ())), preferred_element_type=F32)


def _dot(a, b, precision=None):
    return jnp.dot(a, b, preferred_element_type=F32, precision=precision)


def _ada_kernel(c_ref, w_ref, b_ref, o_ref):
    c = c_ref[...]
    a = (c * jax.nn.sigmoid(c)).astype(BF16)
    o_ref[0] = _dot(a, w_ref[0].astype(BF16)) + b_ref[0]


def _ada_call(c_all, w_ada, b_ada):
    depth, d, n = w_ada.shape
    m = c_all.shape[0]
    tn = 512
    return pl.pallas_call(
        _ada_kernel,
        grid=(depth, n // tn),
        in_specs=[
            pl.BlockSpec((m, d), lambda l, j: (0, 0)),
            pl.BlockSpec((1, d, tn), lambda l, j: (l, 0, j)),
            pl.BlockSpec((1, 1, tn), lambda l, j: (l, 0, j)),
        ],
        out_specs=pl.BlockSpec((1, m, tn), lambda l, j: (l, 0, j)),
        out_shape=jax.ShapeDtypeStruct((depth, m, n), F32),
        compiler_params=_cparams(("arbitrary", "arbitrary")),
        name="ada_mod",
    )(c_all, w_ada, b_ada.reshape(depth, 1, n))


def _cumsum_kernel(x_ref, o_ref, carry_ref):
    @pl.when(pl.program_id(0) == 0)
    def _():
        carry_ref[...] = jnp.zeros_like(carry_ref)

    tb = x_ref.shape[0]
    row = lax.broadcasted_iota(I32, (tb, tb), 0)
    col = lax.broadcasted_iota(I32, (tb, tb), 1)
    tri = jnp.where(col <= row, 1.0, 0.0).astype(F32)
    cs = _dot(tri, x_ref[...], precision=lax.Precision.HIGHEST) + carry_ref[0:1, :]
    o_ref[...] = cs
    carry_ref[...] = jnp.broadcast_to(cs[tb - 1:tb, :], carry_ref.shape)


def _cumsum_call(x):
    s = x.shape[0]
    tb = 256
    return pl.pallas_call(
        _cumsum_kernel,
        grid=(s // tb,),
        in_specs=[pl.BlockSpec((tb, LANES), lambda i: (i, 0))],
        out_specs=pl.BlockSpec((tb, LANES), lambda i: (i, 0)),
        out_shape=jax.ShapeDtypeStruct((s, LANES), F32),
        scratch_shapes=[pltpu.VMEM((8, LANES), F32)],
        compiler_params=_cparams(("arbitrary",)),
        name="forget_cumsum",
    )(x)


_PROJ_PIECES = (
    ("fq", 0, 3, None, False, True),
    ("fk", 3, 3, None, True, True),
    ("fv", 6, 3, None, True, True),
    ("dq", 9, 3, "r128", False, True),
    ("dk", 12, 3, "r128", True, True),
    ("dv", 15, 3, None, True, True),
    ("iq", 18, 8, "r128", False, True),
    ("cq", 26, 2, "r64", False, True),
    ("ck", 28, 2, "r64", True, True),
    ("cv", 30, 2, None, True, True),
)
_PROJ_LAST_TILE = 32
_PROJ_TILES = 33


def _rms_mod(x, g, sc, sh):
    y = x * lax.rsqrt(jnp.mean(x * x, axis=-1, keepdims=True) + RMS_EPS)
    return (y * g) * (1.0 + sc) + sh


def _rope_tile(z, c, s1, s2, shift):
    return z * c + pltpu.roll(z, LANES - shift, 1) * s1 + pltpu.roll(z, shift, 1) * s2


def _proj_kernel(x_ref, sc_ref, sh_ref, g_ref, w_ref, c128, s1128, s2128, c64, s164, s264, bs_ref,
                 *rest, iw_scale):
    outs = rest[:-1]
    h_s = rest[-1]
    j = pl.program_id(1)

    @pl.when(j == 0)
    def _():
        h_s[...] = _rms_mod(x_ref[...], g_ref[...], sc_ref[0], sh_ref[0]).astype(BF16)

    z = _dot(h_s[...], w_ref[...])

    def rope(zz, kind):
        if kind is None:
            return zz
        tabs, shift = ((c128, s1128, s2128), 16) if kind == "r128" else ((c64, s164, s264), 8)
        c, s1, s2 = (t[...] for t in tabs)
        return jnp.concatenate(
            [_rope_tile(zz[:, t * LANES:(t + 1) * LANES], c, s1, s2, shift) for t in range(2)], axis=1)

    k = 0
    for (_, start, cnt, kind, wf, wb) in _PROJ_PIECES:
        refs = outs[k:k + int(wf) + int(wb)]
        k += int(wf) + int(wb)

        @pl.when((j >= start) & (j < start + cnt))
        def _(refs=refs, kind=kind):
            r = rope(z, kind)
            for o in refs:
                o[...] = r.astype(o.dtype)

    ik_f, ik_b, small = outs[k:k + 3]

    @pl.when(j == _PROJ_LAST_TILE)
    def _():
        r = _rope_tile(z[:, :LANES], c128[...], s1128[...], s2128[...], 16)
        ik_f[...] = r
        ik_b[...] = r.astype(BF16)
        zs = z[:, LANES:]
        lane = lax.broadcasted_iota(I32, zs.shape, 1)
        zf = zs + bs_ref[...]
        logsig = jnp.minimum(zf, 0.0) - jnp.log(1.0 + jnp.exp(-jnp.abs(zf)))
        small[...] = jnp.where(lane < N_SMALL_FORGET, logsig,
                               jnp.where(lane < N_SMALL_FORGET + N_SMALL_IW, zs * iw_scale, zs))


def _proj_call(x2d, sc, sh, g, w_cat, tabs, b_small, tm, tiles_per_mod):
    n, d = x2d.shape
    r = sc.shape[1]
    ntab = tabs[0].shape[0] // tm
    out_shapes, out_specs = [], []

    def piece_spec(start, cnt, width):
        return pl.BlockSpec((tm, width), lambda i, j: (i, jnp.clip(j - start, 0, cnt - 1)))

    for (_, start, cnt, _, wf, wb) in _PROJ_PIECES:
        for dt, on in ((F32, wf), (BF16, wb)):
            if on:
                out_shapes.append(jax.ShapeDtypeStruct((n, cnt * PROJ_TN), dt))
                out_specs.append(piece_spec(start, cnt, PROJ_TN))
    for dt in (F32, BF16, F32):
        out_shapes.append(jax.ShapeDtypeStruct((n, LANES), dt))
        out_specs.append(pl.BlockSpec((tm, LANES), lambda i, j: (i, 0)))

    mod_spec = pl.BlockSpec((1, r, d), lambda i, j: (i // tiles_per_mod, 0, 0))
    tab_spec = pl.BlockSpec((tm, LANES), lambda i, j: (i % ntab, 0))
    outs = pl.pallas_call(
        functools.partial(_proj_kernel, iw_scale=float(IDX_HEADS ** -0.5 * IDX_DIM ** -0.5)),
        grid=(n // tm, _PROJ_TILES),
        in_specs=[
            pl.BlockSpec((tm, d), lambda i, j: (i, 0)),
            mod_spec, mod_spec,
            pl.BlockSpec((1, d), lambda i, j: (0, 0)),
            pl.BlockSpec((d, PROJ_TN), lambda i, j: (0, j)),
            tab_spec, tab_spec, tab_spec, tab_spec, tab_spec, tab_spec,
            pl.BlockSpec((1, LANES), lambda i, j: (0, 0)),
        ],
        out_specs=out_specs,
        out_shape=out_shapes,
        scratch_shapes=[pltpu.VMEM((tm, d), BF16)],
        compiler_params=_cparams(("arbitrary", "arbitrary")),
        name="norm_in_proj",
    )(x2d, sc, sh, g, w_cat, *tabs, b_small)
    names = []
    for (nm, _, _, _, wf, wb) in _PROJ_PIECES:
        if wf:
            names.append(nm + "_f")
        if wb:
            names.append(nm + "_b")
    names += ["ik_f", "ik_b", "small"]
    return dict(zip(names, outs))


def _rope_tables(pos):
    posf = pos.astype(F32)[:, None]
    lane = jnp.arange(LANES)
    tabs = []
    for hd in (HEAD_DIM, C_QK_DIM):
        half = hd // ROPE_FRACTION // 2
        inv_freq = ROPE_THETA ** (-jnp.arange(half, dtype=F32) / half)
        ang = posf * inv_freq[None, :]
        cos, sin = jnp.cos(ang), jnp.sin(ang)
        off = lane % hd
        first = off < half
        second = (off >= half) & (off < 2 * half)
        idx = jnp.where(first, off, jnp.where(second, off - half, 0))
        cos_l, sin_l = cos[:, idx], sin[:, idx]
        tabs.append(jnp.where((first | second)[None, :], cos_l, 1.0))
        tabs.append(jnp.where(first[None, :], -sin_l, 0.0))
        tabs.append(jnp.where(second[None, :], sin_l, 0.0))
    return tabs


def _chunk_limit(q_pos, s_valid):
    return jnp.minimum((q_pos // CHUNK + 1) * CHUNK, s_valid)


def _last_block(iq, mode, q_pos0, tq, tk, s_valid):
    q_last = q_pos0 + (iq + 1) * tq - 1
    if mode == "fox":
        k_last = jnp.minimum(q_last, s_valid - 1)
    else:
        k_last = _chunk_limit(q_last, s_valid) - 1
    return k_last // tk


def _online_softmax_step(s, v, m_ref, l_ref, acc_ref):
    m_prev = m_ref[...]
    m_next = jnp.maximum(m_prev, jnp.max(s, axis=1, keepdims=True))
    alpha = jnp.exp(m_prev - m_next)
    p = jnp.exp(s - m_next[:, :1])
    l_ref[...] = alpha * l_ref[...] + jnp.sum(p, axis=1, keepdims=True)
    acc_ref[...] = acc_ref[...] * alpha + _dot(p.astype(v.dtype), v)
    m_ref[...] = m_next


def _flash_kernel(*refs, mode, q_pos0, s_valid, tq, tk, scale):
    if mode == "fox":
        q_ref, k_ref, v_ref, cq_ref, ck_ref, o_ref, m_ref, l_ref, acc_ref, fq_ref = refs
    else:
        q_ref, k_ref, v_ref, bias_ref, o_ref, m_ref, l_ref, acc_ref = refs
    b, h, iq, ik = (pl.program_id(a) for a in range(4))
    nk = pl.num_programs(3)
    n_heads = pl.num_programs(1)

    @pl.when(ik == 0)
    def _():
        m_ref[...] = jnp.full_like(m_ref, NEG_INF)
        l_ref[...] = jnp.zeros_like(l_ref)
        acc_ref[...] = jnp.zeros_like(acc_ref)
        if mode == "fox":
            lane = lax.broadcasted_iota(I32, cq_ref.shape, 1)
            col = jnp.sum(jnp.where(lane == b * n_heads + h, cq_ref[...], 0.0), axis=1, keepdims=True)
            fq_ref[...] = jnp.broadcast_to(col, fq_ref.shape)

    last = _last_block(iq, mode, q_pos0, tq, tk, s_valid)

    def step(masked):
        s = _dot_nt(q_ref[0], k_ref[0]) * scale
        if mode == "fox":
            ch = b * n_heads + h
            fk = ck_ref[pl.ds(ch % 8, 1), :]
            s = s + fq_ref[:, :1] - fk
            if masked:
                q_pos = q_pos0 + iq * tq + lax.broadcasted_iota(I32, (tq, tk), 0)
                k_pos = ik * tk + lax.broadcasted_iota(I32, (tq, tk), 1)
                s = jnp.where(k_pos <= q_pos, s, NEG_INF)
        else:
            s = s + bias_ref[0].astype(F32)
        _online_softmax_step(s, v_ref[0], m_ref, l_ref, acc_ref)

    if mode == "fox":
        first_masked = (q_pos0 + iq * tq) // tk

        @pl.when(ik < jnp.minimum(first_masked, last + 1))
        def _():
            step(False)

        @pl.when((ik >= first_masked) & (ik <= last))
        def _():
            step(True)
    else:
        @pl.when(ik <= last)
        def _():
            step(False)

    @pl.when(ik == nk - 1)
    def _():
        o_ref[0] = (acc_ref[...] / l_ref[...]).astype(o_ref.dtype)


def _flash_call(mode, q, k, v, extra, n_heads, q_pos0, s_valid, tq, tk):
    bsz, t_q, _ = q.shape
    s_pad = k.shape[1]
    nq, nk = t_q // tq, s_pad // tk
    last = functools.partial(_last_block, mode=mode, q_pos0=q_pos0, tq=tq, tk=tk, s_valid=s_valid)

    def kv_map(b, h, iq, ik):
        return (b, jnp.minimum(ik, last(iq)), h)

    in_specs = [
        pl.BlockSpec((1, tq, HEAD_DIM), lambda b, h, iq, ik: (b, iq, h)),
        pl.BlockSpec((1, tk, HEAD_DIM), kv_map),
        pl.BlockSpec((1, tk, HEAD_DIM), kv_map),
    ]
    scratch = [pltpu.VMEM((tq, LANES), F32), pltpu.VMEM((tq, LANES), F32), pltpu.VMEM((tq, HEAD_DIM), F32)]
    if mode == "fox":
        cum, cum_t = extra
        qb0 = q_pos0 // tq
        in_specs += [
            pl.BlockSpec((tq, LANES), lambda b, h, iq, ik: (qb0 + iq, 0)),
            pl.BlockSpec((8, tk), lambda b, h, iq, ik: ((b * n_heads + h) // 8, jnp.minimum(ik, last(iq)))),
        ]
        scratch.append(pltpu.VMEM((tq, LANES), F32))
        args = (q, k, v, cum, cum_t)
    else:
        (bias,) = extra
        in_specs.append(pl.BlockSpec((1, tq, tk), lambda b, h, iq, ik: (b, iq, jnp.minimum(ik, last(iq)))))
        args = (q, k, v, bias)
    return pl.pallas_call(
        functools.partial(_flash_kernel, mode=mode, q_pos0=q_pos0, s_valid=s_valid, tq=tq, tk=tk,
                          scale=float(HEAD_DIM ** -0.5)),
        grid=(bsz, n_heads, nq, nk),
        in_specs=in_specs,
        out_specs=pl.BlockSpec((1, tq, HEAD_DIM), lambda b, h, iq, ik: (b, iq, h)),
        out_shape=jax.ShapeDtypeStruct((bsz, t_q, n_heads * HEAD_DIM), BF16),
        scratch_shapes=scratch,
        compiler_params=_cparams(("arbitrary",) * 4),
        name=mode + "_attention",
    )(*args)


def _float_key(x):
    bits = pltpu.bitcast(x + 0.0, I32)
    return jnp.where(bits < 0, bits ^ jnp.int32(0x7FFFFFFF), bits)


def _index_kernel(iq_ref, iw_ref, ik_ref, o_ref, key_ref, wrep_ref, *, q_pos0, s_valid, tq, ck, top_k):
    iq = pl.program_id(1)
    s_pad = ik_ref.shape[1]
    q_pos = q_pos0 + iq * tq + lax.broadcasted_iota(I32, (tq, 1), 0)
    limit = _chunk_limit(q_pos, s_valid)
    n_adm = _chunk_limit(q_pos0 + (iq + 1) * tq - 1, s_valid)
    n_chunks = (n_adm + ck - 1) // ck

    iw = iw_ref[0]
    for h in range(IDX_HEADS):
        wrep_ref[h] = jnp.broadcast_to(iw[:, N_SMALL_FORGET + h:N_SMALL_FORGET + h + 1], (tq, LANES))
    n_lt = ck // LANES

    def score_chunk(c, _):
        off = pl.multiple_of(c * ck, ck)
        ikc = ik_ref[0, pl.ds(off, ck), :]
        tiles = [jnp.zeros((tq, LANES), F32)] * n_lt
        for h in range(IDX_HEADS):
            rel = _dot_nt(iq_ref[0, :, h * IDX_DIM:(h + 1) * IDX_DIM], ikc)
            wh = wrep_ref[h]
            tiles = [tiles[t] + jnp.maximum(rel[:, t * LANES:(t + 1) * LANES], 0.0) * wh for t in range(n_lt)]
        score = jnp.concatenate(tiles, axis=1)
        k_pos = off + lax.broadcasted_iota(I32, (tq, ck), 1)
        key_ref[:, pl.ds(off, ck)] = jnp.where(k_pos < limit, _float_key(score), INT_MIN)
        return 0

    lax.fori_loop(0, n_chunks, score_chunk, 0)

    def count_ge(cand):
        def body(c, cnt):
            off = pl.multiple_of(c * ck, ck)
            kc = key_ref[:, pl.ds(off, ck)]
            for t in range(ck // LANES):
                cnt = cnt + jnp.where(kc[:, t * LANES:(t + 1) * LANES] >= cand, 1, 0)
            return cnt
        cnt = lax.fori_loop(0, n_chunks, body, jnp.zeros((tq, LANES), I32))
        return jnp.sum(cnt, axis=1, keepdims=True)

    def bit_step(i, lo):
        cand = lo + lax.shift_left(jnp.int32(1), 31 - i)
        return jnp.where(count_ge(cand) >= top_k, cand, lo)

    lo = lax.fori_loop(0, 32, bit_step, jnp.full((tq, LANES), INT_MIN, I32))
    thr = jnp.maximum(lo, INT_MIN + 1)

    o_ref[0] = jnp.full((tq, s_pad), NEG_INF, o_ref.dtype)

    def write_chunk(c, _):
        off = pl.multiple_of(c * ck, ck)
        kc = key_ref[:, pl.ds(off, ck)]
        o_ref[0, :, pl.ds(off, ck)] = jnp.where(kc >= thr[:, :1], 0.0, NEG_INF).astype(o_ref.dtype)
        return 0

    lax.fori_loop(0, n_chunks, write_chunk, 0)


def _index_call(iq, iw, ik, q_pos0, s_valid, tq, ck):
    bsz, t_q, _ = iq.shape
    s_pad = ik.shape[1]
    top_k = min(IDX_TOPK, s_valid // 4)
    return pl.pallas_call(
        functools.partial(_index_kernel, q_pos0=q_pos0, s_valid=s_valid, tq=tq, ck=ck, top_k=top_k),
        grid=(bsz, t_q // tq),
        in_specs=[
            pl.BlockSpec((1, tq, IDX_WIDTH), lambda b, i: (b, i, 0)),
            pl.BlockSpec((1, tq, LANES), lambda b, i: (b, i, 0)),
            pl.BlockSpec((1, s_pad, IDX_DIM), lambda b, i: (b, 0, 0)),
        ],
        out_specs=pl.BlockSpec((1, tq, s_pad), lambda b, i: (b, i, 0)),
        out_shape=jax.ShapeDtypeStruct((bsz, t_q, s_pad), BF16),
        scratch_shapes=[pltpu.VMEM((tq, s_pad), I32), pltpu.VMEM((IDX_HEADS, tq, LANES), F32)],
        compiler_params=_cparams(("arbitrary", "arbitrary")),
        name="indexer_topk_mask",
    )(iq, iw, ik)


def _diff_kernel(q_ref, k_ref, v_ref, dl_ref, g_ref, o_ref, m0, l0, a0, m1, l1, a1,
                 *, q_pos0, s_valid, tq, tk, lam_init):
    iq, ik = pl.program_id(2), pl.program_id(3)
    nk = pl.num_programs(3)

    @pl.when(ik == 0)
    def _():
        for m_ref, l_ref, a_ref in ((m0, l0, a0), (m1, l1, a1)):
            m_ref[...] = jnp.full_like(m_ref, NEG_INF)
            l_ref[...] = jnp.zeros_like(l_ref)
            a_ref[...] = jnp.zeros_like(a_ref)

    last = _last_block(iq, "chunk", q_pos0, tq, tk, s_valid)

    @pl.when(ik <= last)
    def _():
        q = q_ref[0]
        k = k_ref[0]
        v = v_ref[0]
        lane = lax.broadcasted_iota(I32, q.shape, 1)
        q_pos = q_pos0 + iq * tq + lax.broadcasted_iota(I32, (tq, 1), 0)
        k_pos = ik * tk + lax.broadcasted_iota(I32, (tq, tk), 1)
        mask = k_pos < _chunk_limit(q_pos, s_valid)
        for sel, (m_ref, l_ref, a_ref) in ((lane < C_QK_DIM, (m0, l0, a0)), (lane >= C_QK_DIM, (m1, l1, a1))):
            qm = jnp.where(sel, q, jnp.zeros_like(q))
            s = _dot_nt(qm, k) * float(C_QK_DIM ** -0.5)
            _online_softmax_step(jnp.where(mask, s, NEG_INF), v, m_ref, l_ref, a_ref)

    @pl.when(ik == nk - 1)
    def _():
        dl = dl_ref[...]
        lam = (jnp.exp(jnp.sum(dl[0:1] * dl[1:2], axis=1, keepdims=True))
               - jnp.exp(jnp.sum(dl[2:3] * dl[3:4], axis=1, keepdims=True)) + lam_init)
        o = a0[...] / l0[...] - lam * (a1[...] / l1[...])
        o = o * lax.rsqrt(jnp.mean(o * o, axis=1, keepdims=True) + SUBLN_EPS)
        o_ref[0] = (o * g_ref[...] * (1.0 - lam_init)).astype(o_ref.dtype)


def _diff_call(q, k, v, dl, g, q_pos0, s_valid, tq, tk, lam_init):
    bsz, t_q, _ = q.shape
    s_pad = k.shape[1]
    last = functools.partial(_last_block, mode="chunk", q_pos0=q_pos0, tq=tq, tk=tk, s_valid=s_valid)

    def kv_map(b, h, iq, ik):
        return (b, jnp.minimum(ik, last(iq)), h)

    vec = lambda: pltpu.VMEM((tq, LANES), F32)
    return pl.pallas_call(
        functools.partial(_diff_kernel, q_pos0=q_pos0, s_valid=s_valid, tq=tq, tk=tk, lam_init=lam_init),
        grid=(bsz, C_HEADS, t_q // tq, s_pad // tk),
        in_specs=[
            pl.BlockSpec((1, tq, LANES), lambda b, h, iq, ik: (b, iq, h)),
            pl.BlockSpec((1, tk, LANES), kv_map),
            pl.BlockSpec((1, tk, LANES), kv_map),
            pl.BlockSpec((4, C_QK_DIM), lambda b, h, iq, ik: (0, 0)),
            pl.BlockSpec((1, LANES), lambda b, h, iq, ik: (0, 0)),
        ],
        out_specs=pl.BlockSpec((1, tq, LANES), lambda b, h, iq, ik: (b, iq, h)),
        out_shape=jax.ShapeDtypeStruct((bsz, t_q, C_WIDTH), BF16),
        scratch_shapes=[vec(), vec(), vec(), vec(), vec(), vec()],
        compiler_params=_cparams(("arbitrary",) * 4),
        name="diff_attention",
    )(q, k, v, dl, g)


def _out_kernel(x_ref, oa_ref, ob_ref, oc_ref, w_ref, gt_ref, sc_ref, sh_ref, g_ref, wr_ref, br_ref,
                x1_ref, h_ref, ti_ref, tw_ref):
    attn = (_dot(oa_ref[...], w_ref[0:A_WIDTH, :])
            + _dot(ob_ref[...], w_ref[A_WIDTH:A_WIDTH + B_WIDTH, :])
            + _dot(oc_ref[...], w_ref[A_WIDTH + B_WIDTH:, :]))
    x1 = x_ref[...] + gt_ref[0] * attn
    x1_ref[...] = x1
    h = _rms_mod(x1, g_ref[...], sc_ref[0], sh_ref[0])
    h_ref[...] = h
    logits = _dot(h, wr_ref[...], precision=lax.Precision.HIGHEST) + br_ref[...]
    lane = lax.broadcasted_iota(I32, logits.shape, 1)
    vals, idxs = [], []
    for _ in range(TOP_K):
        mx = jnp.max(logits, axis=1, keepdims=True)
        ix = jnp.min(jnp.where(logits == mx, lane, LANES), axis=1, keepdims=True)
        vals.append(mx)
        idxs.append(ix)
        logits = jnp.where(lane == ix, -jnp.inf, logits)
    es = [jnp.exp(v - vals[0]) for v in vals]
    den = es[0] + es[1] + es[2] + es[3]
    ti = jnp.zeros(lane.shape, I32)
    tw = jnp.zeros(lane.shape, F32)
    for kk in range(TOP_K):
        ti = jnp.where(lane == kk, idxs[kk], ti)
        tw = jnp.where(lane == kk, es[kk] / den, tw)
    ti_ref[...] = ti
    tw_ref[...] = tw


def _out_call(x2d, oa, ob, oc, w_out, gt, sc, sh, g, w_router, b_router, tm, tiles_per_mod):
    n, d = x2d.shape
    r = gt.shape[1]
    row = lambda w: pl.BlockSpec((tm, w), lambda i: (i, 0))
    mod_spec = pl.BlockSpec((1, r, d), lambda i: (i // tiles_per_mod, 0, 0))
    full = lambda a: pl.BlockSpec(a.shape, lambda i: (0,) * a.ndim)
    return pl.pallas_call(
        _out_kernel,
        grid=(n // tm,),
        in_specs=[row(d), row(A_WIDTH), row(B_WIDTH), row(C_WIDTH), full(w_out), mod_spec, mod_spec, mod_spec,
                  full(g), full(w_router), full(b_router)],
        out_specs=[row(d), row(d), row(LANES), row(LANES)],
        out_shape=[jax.ShapeDtypeStruct((n, d), F32), jax.ShapeDtypeStruct((n, d), F32),
                   jax.ShapeDtypeStruct((n, LANES), I32), jax.ShapeDtypeStruct((n, LANES), F32)],
        compiler_params=_cparams(("arbitrary",)),
        name="out_proj_router",
    )(x2d, oa, ob, oc, w_out, gt, sc, sh, g, w_router, b_router)


def _row_gather(src_hbm, idx_smem, dst_ref, sem, n_rows):
    def copy(r):
        return pltpu.make_async_copy(src_hbm.at[pl.ds(idx_smem[r], 1)], dst_ref.at[pl.ds(r, 1)], sem)

    def start(r, _):
        copy(r).start()
        return 0

    def wait(r, _):
        copy(r).wait()
        return 0

    lax.fori_loop(0, n_rows, start, 0)
    lax.fori_loop(0, n_rows, wait, 0)


def _gather_kernel(idx_hbm, src_hbm, o_ref, idx_smem, sem_idx, sem_rows):
    i = pl.program_id(0)
    cp = pltpu.make_async_copy(idx_hbm.at[i], idx_smem, sem_idx)
    cp.start()
    cp.wait()
    _row_gather(src_hbm, idx_smem, o_ref, sem_rows, o_ref.shape[0])


def _gather_call(src, idx, tr):
    d = src.shape[1]
    p = idx.shape[0]
    return pl.pallas_call(
        _gather_kernel,
        grid=(p // tr,),
        in_specs=[pl.BlockSpec(memory_space=pl.ANY), pl.BlockSpec(memory_space=pl.ANY)],
        out_specs=pl.BlockSpec((tr, d), lambda i: (i, 0)),
        out_shape=jax.ShapeDtypeStruct((p, d), src.dtype),
        scratch_shapes=[pltpu.SMEM((tr,), I32), pltpu.SemaphoreType.DMA, pltpu.SemaphoreType.DMA],
        compiler_params=_cparams(("arbitrary",)),
        name="moe_row_gather",
    )(idx.reshape(p // tr, tr), src)


def _expert_kernel(te_ref, nt_ref, x_ref, wg_ref, wu_ref, bg_ref, bu_ref, wd_ref, bd_ref, o_ref, xb_ref, acc_ref):
    t, j = pl.program_id(0), pl.program_id(1)
    nf = pl.num_programs(1)
    used = t < nt_ref[0]

    @pl.when(used & (j == 0))
    def _():
        xb_ref[...] = x_ref[...].astype(BF16)
        acc_ref[...] = jnp.zeros_like(acc_ref)

    @pl.when(used)
    def _():
        x = xb_ref[...]
        gate = jnp.minimum(_dot(x, wg_ref[0]) + bg_ref[0], SWIGLU_LIMIT)
        up = jnp.clip(_dot(x, wu_ref[0]) + bu_ref[0], -SWIGLU_LIMIT, SWIGLU_LIMIT)
        act = (up + 1.0) * (gate * jax.nn.sigmoid(gate * SWIGLU_ALPHA))
        acc_ref[...] += _dot(act.astype(BF16), wd_ref[0])

    @pl.when(used & (j == nf - 1))
    def _():
        o_ref[...] = acc_ref[...] + bd_ref[0]

    @pl.when(jnp.logical_not(used) & (j == nf - 1))
    def _():
        o_ref[...] = jnp.zeros_like(o_ref)


def _expert_call(xs, tile_expert, n_tiles_used, w_gu, b_gu, w_d, b_d, tm, tf):
    p, d = xs.shape
    n_exp, _, f2 = w_gu.shape
    f = f2 // 2
    nf = f // tf
    grid_spec = pltpu.PrefetchScalarGridSpec(
        num_scalar_prefetch=2,
        grid=(p // tm, nf),
        in_specs=[
            pl.BlockSpec((tm, d), lambda t, j, te, nt: (t, 0)),
            pl.BlockSpec((1, d, tf), lambda t, j, te, nt: (te[t], 0, j)),
            pl.BlockSpec((1, d, tf), lambda t, j, te, nt: (te[t], 0, nf + j)),
            pl.BlockSpec((1, 1, tf), lambda t, j, te, nt: (te[t], 0, j)),
            pl.BlockSpec((1, 1, tf), lambda t, j, te, nt: (te[t], 0, nf + j)),
            pl.BlockSpec((1, tf, d), lambda t, j, te, nt: (te[t], j, 0)),
            pl.BlockSpec((1, 1, d), lambda t, j, te, nt: (te[t], 0, 0)),
        ],
        out_specs=pl.BlockSpec((tm, d), lambda t, j, te, nt: (t, 0)),
        scratch_shapes=[pltpu.VMEM((tm, d), BF16), pltpu.VMEM((tm, d), F32)],
    )
    return pl.pallas_call(
        _expert_kernel,
        grid_spec=grid_spec,
        out_shape=jax.ShapeDtypeStruct((p, d), F32),
        compiler_params=_cparams(("arbitrary", "arbitrary")),
        name="moe_experts",
    )(tile_expert, n_tiles_used, xs, w_gu, w_gu, b_gu.reshape(n_exp, 1, f2), b_gu.reshape(n_exp, 1, f2),
      w_d, b_d.reshape(n_exp, 1, d))


def _combine_kernel(pos_hbm, y_hbm, x_ref, tw_ref, gt_ref, gf_ref, o_ref, idx_smem, rows_ref, sem_idx, sem_rows,
                    *, final_norm):
    i = pl.program_id(0)
    tn = x_ref.shape[0]
    cp = pltpu.make_async_copy(pos_hbm.at[i], idx_smem, sem_idx)
    cp.start()
    cp.wait()
    _row_gather(y_hbm, idx_smem, rows_ref, sem_rows, TOP_K * tn)
    tw = tw_ref[...]
    moe = jnp.zeros(x_ref.shape, F32)
    for kk in range(TOP_K):
        moe = moe + tw[:, kk:kk + 1] * rows_ref[kk * tn:(kk + 1) * tn, :]
    x2 = x_ref[...] + gt_ref[0] * moe
    if final_norm:
        x2 = x2 * lax.rsqrt(jnp.mean(x2 * x2, axis=-1, keepdims=True) + RMS_EPS) * gf_ref[...]
    o_ref[...] = x2


def _combine_call(pos, y_sorted, x1, top_w, gt, tiles_per_mod, g_final, tn, final_norm):
    n, d = x1.shape
    r = gt.shape[1]
    pos_tiles = pos.reshape(n // tn, tn, TOP_K).transpose(0, 2, 1).reshape(n // tn, TOP_K * tn)
    return pl.pallas_call(
        functools.partial(_combine_kernel, final_norm=final_norm),
        grid=(n // tn,),
        in_specs=[
            pl.BlockSpec(memory_space=pl.ANY), pl.BlockSpec(memory_space=pl.ANY),
            pl.BlockSpec((tn, d), lambda i: (i, 0)),
            pl.BlockSpec((tn, LANES), lambda i: (i, 0)),
            pl.BlockSpec((1, r, d), lambda i: (i // tiles_per_mod, 0, 0)),
            pl.BlockSpec((1, d), lambda i: (0, 0)),
        ],
        out_specs=pl.BlockSpec((tn, d), lambda i: (i, 0)),
        out_shape=jax.ShapeDtypeStruct((n, d), F32),
        scratch_shapes=[pltpu.SMEM((TOP_K * tn,), I32), pltpu.VMEM((TOP_K * tn, d), F32),
                        pltpu.SemaphoreType.DMA, pltpu.SemaphoreType.DMA],
        compiler_params=_cparams(("arbitrary",)),
        name="moe_combine",
    )(pos_tiles, y_sorted, x1, top_w, gt, g_final)


def _routing_plan(top_i, n_exp, tm):
    n = top_i.shape[0]
    flat = top_i.reshape(-1)
    order = jnp.argsort(flat, stable=True).astype(I32)
    sorted_e = flat[order]
    counts = jnp.zeros((n_exp,), I32).at[flat].add(1)
    padded = (counts + tm - 1) // tm * tm
    start = jnp.cumsum(counts) - counts
    pstart = jnp.cumsum(padded) - padded
    dest = pstart[sorted_e] + jnp.arange(n * TOP_K, dtype=I32) - start[sorted_e]
    p = (n * TOP_K + n_exp * (tm - 1)) // tm * tm
    src_row = jnp.zeros((p,), I32).at[dest].set(order // TOP_K)
    pos = jnp.zeros((n * TOP_K,), I32).at[order].set(dest).reshape(n, TOP_K)
    tile_start = jnp.arange(p // tm, dtype=I32) * tm
    tile_expert = jnp.minimum(jnp.searchsorted(jnp.cumsum(padded), tile_start, side="right"), n_exp - 1).astype(I32)
    n_tiles_used = (jnp.sum(padded) // tm).astype(I32).reshape(1)
    return src_row, pos, tile_expert, n_tiles_used


def _pad_rows(a, axis, size):
    pad = size - a.shape[axis]
    if pad == 0:
        return a
    widths = [(0, 0)] * a.ndim
    widths[axis] = (0, pad)
    return jnp.pad(a, widths)


def _mixers(pr, new_logf, past, q_pos0, dl, g_diff, lam_init, tq, tk, tq_idx):
    bsz, t = pr["fq_b"].shape[:2]
    keys = {}
    if past is None:
        for nm in ("fk", "fv", "dk", "dv", "ik", "ck", "cv"):
            keys[nm] = pr[nm + "_b"]
        logf_all = new_logf
        s_valid = t
    else:
        s_valid = past["fk"].shape[1] + t
        s_pad = -(-s_valid // tk) * tk
        for nm in ("fk", "fv", "dk", "dv", "ik", "ck", "cv"):
            cache = past[nm].reshape(bsz, past[nm].shape[1], -1).astype(BF16)
            keys[nm] = _pad_rows(jnp.concatenate([cache, pr[nm + "_b"]], axis=1), 1, s_pad)
        logf_all = jnp.concatenate([past["logf"], new_logf], axis=1)
    s_pad = keys["fk"].shape[1]

    chan = jnp.transpose(logf_all, (1, 0, 2)).reshape(logf_all.shape[1], bsz * A_HEADS)
    cum = _cumsum_call(_pad_rows(_pad_rows(chan, 1, LANES), 0, s_pad))
    o_a = _flash_call("fox", pr["fq_b"], keys["fk"], keys["fv"], (cum, cum.T), A_HEADS, q_pos0, s_valid, tq, tk)

    bias = _index_call(pr["iq_b"], pr["small"], keys["ik"], q_pos0, s_valid, tq_idx, tk)
    o_b = _flash_call("dsa", pr["dq_b"], keys["dk"], keys["dv"], (bias,), B_HEADS, q_pos0, s_valid, tq, tk)

    o_c = _diff_call(pr["cq_b"], keys["ck"], keys["cv"], dl, g_diff, q_pos0, s_valid, tq, tk, lam_init)
    return o_a, o_b, o_c


def kernel(x_prompt, x_sample, cache_fox_k, cache_fox_v, cache_fox_logf, cache_dsa_k, cache_dsa_v,
           cache_dsa_idx_k, cache_diff_k, cache_diff_v, c_prompt, c_sample, w_ada, b_ada, g_attn, g_ffn,
           w_in, b_forget, diff_lambda, g_diff, w_out, w_router, b_router, w_gate_up, b_gate_up,
           w_down, b_down, g_final):
    depth = w_ada.shape[0]
    bp, tp, d = x_prompt.shape
    bs, ts, _ = x_sample.shape
    past_len = cache_fox_k.shape[2]
    n_exp = w_router.shape[-1]
    np_, ns = bp * tp, bs * ts

    tm_p = min(512, tp)
    tm_o = min(512, tp)
    tq = min(512, tp)
    tk = min(512, tp)
    tq_idx = min(128, tp)
    tm_e = 512
    tn_c = min(256, ns)

    m_rows = -(-(bp + bs) // 16) * 16
    c_all = _pad_rows(jnp.concatenate([c_prompt, c_sample], axis=0), 0, m_rows)
    mod = _ada_call(c_all, w_ada, b_ada)

    tabs_p = _rope_tables(jnp.arange(tp, dtype=I32))
    tabs_s = [jnp.tile(tb, (bs, 1)) for tb in _rope_tables(jnp.arange(past_len, past_len + ts, dtype=I32))]

    offs = {}
    acc = 0
    for nm, wdt in (("fq", A_WIDTH), ("fk", A_WIDTH), ("fv", A_WIDTH), ("ff", A_HEADS), ("dq", B_WIDTH),
                    ("dk", B_WIDTH), ("dv", B_WIDTH), ("iq", IDX_WIDTH), ("ik", IDX_DIM), ("iw", IDX_HEADS),
                    ("cq", C_WIDTH), ("ck", C_WIDTH), ("cv", C_WIDTH)):
        offs[nm] = (acc, acc + wdt)
        acc += wdt
    col_order = ("fq", "fk", "fv", "dq", "dk", "dv", "iq", "cq", "ck", "cv", "ik", "ff", "iw")

    xp = x_prompt.reshape(np_, d)
    xs = x_sample.reshape(ns, d)
    rows_p, rows_s = [], []
    for l in range(depth):
        lam_init = 0.8 - 0.6 * math.exp(-0.3 * l)
        w_cat = jnp.concatenate([w_in[l][:, offs[nm][0]:offs[nm][1]] for nm in col_order], axis=1)
        w_cat = _pad_rows(w_cat, 1, _PROJ_TILES * PROJ_TN).astype(BF16)
        b_small = _pad_rows(b_forget[l].reshape(1, A_HEADS), 1, LANES)
        w_out_b = w_out[l].astype(BF16)
        w_r = _pad_rows(w_router[l], 1, LANES)
        b_r = jnp.concatenate([b_router[l], jnp.full((LANES - n_exp,), NEG_INF, F32)]).reshape(1, LANES)
        w_gu_b = w_gate_up[l].astype(BF16)
        w_d_b = w_down[l].astype(BF16)

        mods_p = [m.reshape(bp, 1, d) for m in jnp.split(mod[l, :bp], 6, axis=-1)]
        mods_s = [jnp.broadcast_to(m[:, None, :], (bs, ts, d)).reshape(ns, d)
                  for m in jnp.split(mod[l, bp:bp + bs], 6, axis=-1)]

        def mod_view(m, t, tile):
            if m.ndim == 3:
                return m, t // tile
            return m.reshape(m.shape[0] // tile, tile, d), 1

        groups = []
        for (x2d, mods, tabs, bsz, t, tm, past, q_pos0, tqg, tqi) in (
                (xp, mods_p, tabs_p, bp, tp, tm_p, None, 0, tq, tq_idx),
                (xs, mods_s, tabs_s, bs, ts, ns,
                 dict(fk=cache_fox_k[l], fv=cache_fox_v[l], logf=cache_fox_logf[l], dk=cache_dsa_k[l],
                      dv=cache_dsa_v[l], ik=cache_dsa_idx_k[l], ck=cache_diff_k[l], cv=cache_diff_v[l]),
                 past_len, ts, ts)):
            sh1, sc1, gt1, sh2, sc2, gt2 = mods
            (sc1v, tpm), (sh1v, _) = mod_view(sc1, t, tm), mod_view(sh1, t, tm)
            pr = _proj_call(x2d, sc1v, sh1v, g_attn[l].reshape(1, d), w_cat, tabs, b_small, tm, tpm)
            pr3 = {nm: a.reshape(bsz, t, a.shape[-1]) for nm, a in pr.items()}
            new_logf = pr3["small"][:, :, :A_HEADS]
            o_a, o_b, o_c = _mixers(pr3, new_logf, past, q_pos0, diff_lambda[l], g_diff[l].reshape(1, LANES),
                                    lam_init, tqg, tk, tqi)
            tmo = min(tm_o, bsz * t)
            (gt1v, tpm_o), (sc2v, _), (sh2v, _) = (mod_view(m, t, tmo) for m in (gt1, sc2, sh2))
            x1, h2, top_i, top_w = _out_call(
                x2d, o_a.reshape(bsz * t, -1), o_b.reshape(bsz * t, -1), o_c.reshape(bsz * t, -1), w_out_b,
                gt1v, sc2v, sh2v, g_ffn[l].reshape(1, d), w_r, b_r, tmo, tpm_o)
            rows = (pr3["fk_f"].reshape(bsz, t, A_HEADS, HEAD_DIM), pr3["fv_f"].reshape(bsz, t, A_HEADS, HEAD_DIM),
                    new_logf, pr3["dk_f"].reshape(bsz, t, B_HEADS, HEAD_DIM),
                    pr3["dv_f"].reshape(bsz, t, B_HEADS, HEAD_DIM), pr3["ik_f"],
                    pr3["ck_f"].reshape(bsz, t, C_HEADS, 2, C_QK_DIM), pr3["cv_f"].reshape(bsz, t, C_HEADS, 2 * C_QK_DIM))
            groups.append((x1, h2, top_i, top_w, gt2, rows))

        (x1p, h2p, tip, twp, gt2p, rp), (x1s, h2s, tis, tws, gt2s, rs) = groups
        rows_p.append(rp)
        rows_s.append(rs)

        h_all = jnp.concatenate([h2p, h2s], axis=0)
        ti_all = jnp.concatenate([tip[:, :TOP_K], tis[:, :TOP_K]], axis=0)
        src_row, pos, tile_expert, n_tiles_used = _routing_plan(ti_all, n_exp, tm_e)
        x_sorted = _gather_call(h_all, src_row, tm_e)
        y_sorted = _expert_call(x_sorted, tile_expert, n_tiles_used, w_gu_b, b_gate_up[l], w_d_b, b_down[l],
                                tm_e, min(512, w_d_b.shape[1]))
        final = l == depth - 1
        gf = g_final.reshape(1, d)
        xp = _combine_call(pos[:np_], y_sorted, x1p, twp, *mod_view(gt2p, tp, tn_c), gf, tn_c, final)
        xs = _combine_call(pos[np_:], y_sorted, x1s, tws, *mod_view(gt2s, ts, tn_c), gf, tn_c, final)

    def stack(rows, i):
        return jnp.stack([r[i] for r in rows], axis=0)

    return ((xp.reshape(bp, tp, d), xs.reshape(bs, ts, d))
            + tuple(stack(rows_p, i) for i in range(8)) + tuple(stack(rows_s, i) for i in range(8)))
```

```python
import functools
import math

import jax
import jax.numpy as jnp
from jax import lax
from jax.experimental import pallas as pl
from jax.experimental.pallas import tpu as pltpu

F32 = jnp.float32
BF16 = jnp.bfloat16
I32 = jnp.int32

LANES = 128
HEAD_DIM = 128
A_HEADS = 6
B_HEADS = 6
C_HEADS = 4
C_QK_DIM = 64
IDX_HEADS = 16
IDX_DIM = 128
IDX_TOPK = 256
CHUNK = 64
ROPE_THETA = 500000.0
ROPE_FRACTION = 4
TOP_K = 4
SWIGLU_LIMIT = 7.0
SWIGLU_ALPHA = 1.702
RMS_EPS = 1e-6
SUBLN_EPS = 1e-5
NEG_INF = -1e30
INT_MIN = -2 ** 31
LOG2E = 1.4426950408889634
HEADS_PER_STEP = 2

A_WIDTH = A_HEADS * HEAD_DIM
B_WIDTH = B_HEADS * HEAD_DIM
C_WIDTH = C_HEADS * 2 * C_QK_DIM
IDX_WIDTH = IDX_HEADS * IDX_DIM
N_SMALL_FORGET = A_HEADS
N_SMALL_IW = IDX_HEADS

PROJ_TN = 256
VMEM_LIMIT = 56 * 1024 * 1024


def _cparams(sem):
    return pltpu.CompilerParams(dimension_semantics=sem, vmem_limit_bytes=VMEM_LIMIT)


def _dot_nt(a, b):
    return lax.dot_general(a, b, (((1,), (1,)), ((), ())), preferred_element_type=F32)


def _dot(a, b, precision=None):
    return jnp.dot(a, b, preferred_element_type=F32, precision=precision)


def _ada_kernel(c_ref, w_ref, b_ref, o_ref):
    c = c_ref[...]
    a = (c * jax.nn.sigmoid(c)).astype(BF16)
    o_ref[0] = _dot(a, w_ref[0].astype(BF16)) + b_ref[0]


def _ada_call(c_all, w_ada, b_ada):
    depth, d, n = w_ada.shape
    m = c_all.shape[0]
    tn = 512
    return pl.pallas_call(
        _ada_kernel,
        grid=(depth, n // tn),
        in_specs=[
            pl.BlockSpec((m, d), lambda l, j: (0, 0)),
            pl.BlockSpec((1, d, tn), lambda l, j: (l, 0, j)),
            pl.BlockSpec((1, 1, tn), lambda l, j: (l, 0, j)),
        ],
        out_specs=pl.BlockSpec((1, m, tn), lambda l, j: (l, 0, j)),
        out_shape=jax.ShapeDtypeStruct((depth, m, n), F32),
        compiler_params=_cparams(("arbitrary", "arbitrary")),
        name="ada_mod",
    )(c_all, w_ada, b_ada.reshape(depth, 1, n))


def _cumsum_kernel(x_ref, o_ref, carry_ref):
    @pl.when(pl.program_id(0) == 0)
    def _():
        carry_ref[...] = jnp.zeros_like(carry_ref)

    tb = x_ref.shape[0]
    row = lax.broadcasted_iota(I32, (tb, tb), 0)
    col = lax.broadcasted_iota(I32, (tb, tb), 1)
    tri = jnp.where(col <= row, 1.0, 0.0).astype(F32)
    cs = _dot(tri, x_ref[...], precision=lax.Precision.HIGHEST) + carry_ref[0:1, :]
    o_ref[...] = cs
    carry_ref[...] = jnp.broadcast_to(cs[tb - 1:tb, :], carry_ref.shape)


def _cumsum_call(x):
    s = x.shape[0]
    tb = 256
    return pl.pallas_call(
        _cumsum_kernel,
        grid=(s // tb,),
        in_specs=[pl.BlockSpec((tb, LANES), lambda i: (i, 0))],
        out_specs=pl.BlockSpec((tb, LANES), lambda i: (i, 0)),
        out_shape=jax.ShapeDtypeStruct((s, LANES), F32),
        scratch_shapes=[pltpu.VMEM((8, LANES), F32)],
        compiler_params=_cparams(("arbitrary",)),
        name="forget_cumsum",
    )(x)


_PROJ_PIECES = (
    ("fq", 0, 3, None, False, True),
    ("fk", 3, 3, None, True, True),
    ("fv", 6, 3, None, True, True),
    ("dq", 9, 3, "r128", False, True),
    ("dk", 12, 3, "r128", True, True),
    ("dv", 15, 3, None, True, True),
    ("iq", 18, 8, "r128", False, True),
    ("cq", 26, 2, "r64", False, True),
    ("ck", 28, 2, "r64", True, True),
    ("cv", 30, 2, None, True, True),
)
_PROJ_LAST_TILE = 32
_PROJ_TILES = 33


def _rms_mod(x, g, sc, sh):
    y = x * lax.rsqrt(jnp.mean(x * x, axis=-1, keepdims=True) + RMS_EPS)
    return (y * g) * (1.0 + sc) + sh


def _rope_tile(z, c, s1, s2, shift):
    return z * c + pltpu.roll(z, LANES - shift, 1) * s1 + pltpu.roll(z, shift, 1) * s2


def _proj_kernel(x_ref, sc_ref, sh_ref, g_ref, w_ref, c128, s1128, s2128, c64, s164, s264, bs_ref,
                 *rest, iw_scale):
    outs = rest[:-1]
    h_s = rest[-1]
    j = pl.program_id(1)

    @pl.when(j == 0)
    def _():
        h_s[...] = _rms_mod(x_ref[...], g_ref[...], sc_ref[0], sh_ref[0]).astype(BF16)

    z = _dot(h_s[...], w_ref[...])

    def rope(zz, kind):
        if kind is None:
            return zz
        tabs, shift = ((c128, s1128, s2128), 16) if kind == "r128" else ((c64, s164, s264), 8)
        c, s1, s2 = (t[...] for t in tabs)
        return jnp.concatenate(
            [_rope_tile(zz[:, t * LANES:(t + 1) * LANES], c, s1, s2, shift) for t in range(2)], axis=1)

    k = 0
    for (_, start, cnt, kind, wf, wb) in _PROJ_PIECES:
        refs = outs[k:k + int(wf) + int(wb)]
        k += int(wf) + int(wb)

        @pl.when((j >= start) & (j < start + cnt))
        def _(refs=refs, kind=kind):
            r = rope(z, kind)
            for o in refs:
                o[...] = r.astype(o.dtype)

    ik_f, ik_b, small = outs[k:k + 3]

    @pl.when(j == _PROJ_LAST_TILE)
    def _():
        r = _rope_tile(z[:, :LANES], c128[...], s1128[...], s2128[...], 16)
        ik_f[...] = r
        ik_b[...] = r.astype(BF16)
        zs = z[:, LANES:]
        lane = lax.broadcasted_iota(I32, zs.shape, 1)
        zf = zs + bs_ref[...]
        logsig = jnp.minimum(zf, 0.0) - jnp.log(1.0 + jnp.exp(-jnp.abs(zf)))
        small[...] = jnp.where(lane < N_SMALL_FORGET, logsig,
                               jnp.where(lane < N_SMALL_FORGET + N_SMALL_IW, zs * iw_scale, zs))


def _proj_call(x2d, sc, sh, g, w_cat, tabs, b_small, tm, tiles_per_mod):
    n, d = x2d.shape
    r = sc.shape[1]
    ntab = tabs[0].shape[0] // tm
    out_shapes, out_specs = [], []

    def piece_spec(start, cnt, width):
        return pl.BlockSpec((tm, width), lambda i, j: (i, jnp.clip(j - start, 0, cnt - 1)))

    for (_, start, cnt, _, wf, wb) in _PROJ_PIECES:
        for dt, on in ((F32, wf), (BF16, wb)):
            if on:
                out_shapes.append(jax.ShapeDtypeStruct((n, cnt * PROJ_TN), dt))
                out_specs.append(piece_spec(start, cnt, PROJ_TN))
    for dt in (F32, BF16, F32):
        out_shapes.append(jax.ShapeDtypeStruct((n, LANES), dt))
        out_specs.append(pl.BlockSpec((tm, LANES), lambda i, j: (i, 0)))

    mod_spec = pl.BlockSpec((1, r, d), lambda i, j: (i // tiles_per_mod, 0, 0))
    tab_spec = pl.BlockSpec((tm, LANES), lambda i, j: (i % ntab, 0))
    outs = pl.pallas_call(
        functools.partial(_proj_kernel, iw_scale=float(IDX_HEADS ** -0.5 * IDX_DIM ** -0.5)),
        grid=(n // tm, _PROJ_TILES),
        in_specs=[
            pl.BlockSpec((tm, d), lambda i, j: (i, 0)),
            mod_spec, mod_spec,
            pl.BlockSpec((1, d), lambda i, j: (0, 0)),
            pl.BlockSpec((d, PROJ_TN), lambda i, j: (0, j)),
            tab_spec, tab_spec, tab_spec, tab_spec, tab_spec, tab_spec,
            pl.BlockSpec((1, LANES), lambda i, j: (0, 0)),
        ],
        out_specs=out_specs,
        out_shape=out_shapes,
        scratch_shapes=[pltpu.VMEM((tm, d), BF16)],
        compiler_params=_cparams(("arbitrary", "arbitrary")),
        name="norm_in_proj",
    )(x2d, sc, sh, g, w_cat, *tabs, b_small)
    names = []
    for (nm, _, _, _, wf, wb) in _PROJ_PIECES:
        if wf:
            names.append(nm + "_f")
        if wb:
            names.append(nm + "_b")
    names += ["ik_f", "ik_b", "small"]
    return dict(zip(names, outs))


def _rope_tables(pos):
    posf = pos.astype(F32)[:, None]
    lane = jnp.arange(LANES)
    tabs = []
    for hd in (HEAD_DIM, C_QK_DIM):
        half = hd // ROPE_FRACTION // 2
        inv_freq = ROPE_THETA ** (-jnp.arange(half, dtype=F32) / half)
        ang = posf * inv_freq[None, :]
        cos, sin = jnp.cos(ang), jnp.sin(ang)
        off = lane % hd
        first = off < half
        second = (off >= half) & (off < 2 * half)
        idx = jnp.where(first, off, jnp.where(second, off - half, 0))
        cos_l, sin_l = cos[:, idx], sin[:, idx]
        tabs.append(jnp.where((first | second)[None, :], cos_l, 1.0))
        tabs.append(jnp.where(first[None, :], -sin_l, 0.0))
        tabs.append(jnp.where(second[None, :], sin_l, 0.0))
    return tabs


def _chunk_limit(q_pos, s_valid):
    return jnp.minimum((q_pos // CHUNK + 1) * CHUNK, s_valid)


def _last_block(iq, mode, q_pos0, tq, tk, s_valid):
    q_last = q_pos0 + (iq + 1) * tq - 1
    if mode == "fox":
        k_last = jnp.minimum(q_last, s_valid - 1)
    else:
        k_last = _chunk_limit(q_last, s_valid) - 1
    return k_last // tk


def _lane_tiles(x):
    return [x[:, t * LANES:(t + 1) * LANES] for t in range(x.shape[1] // LANES)]


def _add_rep(s, rep):
    return jnp.concatenate([st + rep for st in _lane_tiles(s)], axis=1)


def _with_ones(v):
    return jnp.concatenate([v, jnp.ones_like(v)], axis=1)


def _online_softmax_step(s, v_ext, m_ref, acc_ref, i):
    m_prev = m_ref[i]
    m_next = jnp.maximum(m_prev, jnp.max(s, axis=1, keepdims=True))
    alpha = jnp.exp2(m_prev - m_next)
    p = jnp.concatenate([jnp.exp2(st - m_next) for st in _lane_tiles(s)], axis=1)
    acc_ref[i] = acc_ref[i] * jnp.concatenate([alpha, alpha], axis=1) + _dot(p.astype(v_ext.dtype), v_ext)
    m_ref[i] = m_next


def _softmax_result(acc_ref, i):
    acc = acc_ref[i]
    return acc[:, :HEAD_DIM] / acc[:, HEAD_DIM:]


def _flash_kernel(*refs, mode, q_pos0, s_valid, tq, tk, n_heads):
    if mode == "fox":
        q_ref, k_ref, v_ref, cq_ref, ck_ref, o_ref, m_ref, acc_ref, fq_ref = refs
    else:
        q_ref, k_ref, v_ref, bias_ref, o_ref, m_ref, acc_ref = refs
    b, hp, iq, ik = (pl.program_id(a) for a in range(4))
    nk = pl.num_programs(3)
    ch0 = b * n_heads + hp * HEADS_PER_STEP
    c = float(HEAD_DIM ** -0.5) * LOG2E

    @pl.when(ik == 0)
    def _():
        m_ref[...] = jnp.full_like(m_ref, NEG_INF)
        acc_ref[...] = jnp.zeros_like(acc_ref)
        if mode == "fox":
            lane = lax.broadcasted_iota(I32, cq_ref.shape, 1)
            cq = cq_ref[...]
            for hh in range(HEADS_PER_STEP):
                col = jnp.sum(jnp.where(lane == ch0 + hh, cq, 0.0), axis=1, keepdims=True)
                fq_ref[hh] = jnp.broadcast_to(col * LOG2E, (tq, LANES))

    last = _last_block(iq, mode, q_pos0, tq, tk, s_valid)

    def step(masked):
        if masked:
            q_pos = q_pos0 + iq * tq + lax.broadcasted_iota(I32, (tq, tk), 0)
            k_pos = ik * tk + lax.broadcasted_iota(I32, (tq, tk), 1)
            visible = k_pos <= q_pos
        for hh in range(HEADS_PER_STEP):
            sl = slice(hh * HEAD_DIM, (hh + 1) * HEAD_DIM)
            s = _dot_nt(q_ref[0, :, sl], k_ref[0, :, sl]) * c
            if mode == "fox":
                fk = ck_ref[pl.ds(ch0 % 8 + hh, 1), :] * LOG2E
                s = _add_rep(s, fq_ref[hh]) - fk
                if masked:
                    s = jnp.where(visible, s, NEG_INF)
            else:
                s = s + bias_ref[0].astype(F32)
            _online_softmax_step(s, _with_ones(v_ref[0, :, sl]), m_ref, acc_ref, hh)

    if mode == "fox":
        first_masked = (q_pos0 + iq * tq) // tk

        @pl.when(ik < jnp.minimum(first_masked, last + 1))
        def _():
            step(False)

        @pl.when((ik >= first_masked) & (ik <= last))
        def _():
            step(True)
    else:
        @pl.when(ik <= last)
        def _():
            step(False)

    @pl.when(ik == nk - 1)
    def _():
        o_ref[0] = jnp.concatenate([_softmax_result(acc_ref, hh) for hh in range(HEADS_PER_STEP)],
                                   axis=1).astype(o_ref.dtype)


def _flash_call(mode, q, k, v, extra, n_heads, q_pos0, s_valid, tq, tk):
    bsz, t_q, _ = q.shape
    s_pad = k.shape[1]
    nq, nk = t_q // tq, s_pad // tk
    hps = HEADS_PER_STEP
    assert n_heads % hps == 0 and 8 % hps == 0
    last = functools.partial(_last_block, mode=mode, q_pos0=q_pos0, tq=tq, tk=tk, s_valid=s_valid)

    def kv_map(b, hp, iq, ik):
        return (b, jnp.minimum(ik, last(iq)), hp)

    in_specs = [
        pl.BlockSpec((1, tq, hps * HEAD_DIM), lambda b, hp, iq, ik: (b, iq, hp)),
        pl.BlockSpec((1, tk, hps * HEAD_DIM), kv_map),
        pl.BlockSpec((1, tk, hps * HEAD_DIM), kv_map),
    ]
    scratch = [pltpu.VMEM((hps, tq, LANES), F32), pltpu.VMEM((hps, tq, 2 * HEAD_DIM), F32)]
    if mode == "fox":
        cum, cum_t = extra
        qb0 = q_pos0 // tq
        in_specs += [
            pl.BlockSpec((tq, LANES), lambda b, hp, iq, ik: (qb0 + iq, 0)),
            pl.BlockSpec((8, tk), lambda b, hp, iq, ik: ((b * n_heads + hp * hps) // 8, jnp.minimum(ik, last(iq)))),
        ]
        scratch.append(pltpu.VMEM((hps, tq, LANES), F32))
        args = (q, k, v, cum, cum_t)
    else:
        (bias,) = extra
        in_specs.append(pl.BlockSpec((1, tq, tk), lambda b, hp, iq, ik: (b, iq, jnp.minimum(ik, last(iq)))))
        args = (q, k, v, bias)
    return pl.pallas_call(
        functools.partial(_flash_kernel, mode=mode, q_pos0=q_pos0, s_valid=s_valid, tq=tq, tk=tk, n_heads=n_heads),
        grid=(bsz, n_heads // hps, nq, nk),
        in_specs=in_specs,
        out_specs=pl.BlockSpec((1, tq, hps * HEAD_DIM), lambda b, hp, iq, ik: (b, iq, hp)),
        out_shape=jax.ShapeDtypeStruct((bsz, t_q, n_heads * HEAD_DIM), BF16),
        scratch_shapes=scratch,
        compiler_params=_cparams(("arbitrary",) * 4),
        name=mode + "_attention",
    )(*args)


def _float_key(x):
    bits = pltpu.bitcast(x + 0.0, I32)
    return jnp.where(bits < 0, bits ^ jnp.int32(0x7FFFFFFF), bits)


def _index_kernel(iq_ref, iw_ref, ik_ref, o_ref, key_ref, wrep_ref, *, q_pos0, s_valid, tq, ck, top_k):
    iq = pl.program_id(1)
    s_pad = ik_ref.shape[1]
    q_pos = q_pos0 + iq * tq + lax.broadcasted_iota(I32, (tq, 1), 0)
    limit = _chunk_limit(q_pos, s_valid)
    n_adm = _chunk_limit(q_pos0 + (iq + 1) * tq - 1, s_valid)
    n_chunks = (n_adm + ck - 1) // ck

    iw = iw_ref[0]
    for h in range(IDX_HEADS):
        wrep_ref[h] = jnp.broadcast_to(iw[:, N_SMALL_FORGET + h:N_SMALL_FORGET + h + 1], (tq, LANES))
    n_lt = ck // LANES

    def score_chunk(c, _):
        off = pl.multiple_of(c * ck, ck)
        ikc = ik_ref[0, pl.ds(off, ck), :]
        tiles = [jnp.zeros((tq, LANES), F32)] * n_lt
        for h in range(IDX_HEADS):
            rel = _dot_nt(iq_ref[0, :, h * IDX_DIM:(h + 1) * IDX_DIM], ikc)
            wh = wrep_ref[h]
            tiles = [tiles[t] + jnp.maximum(rel[:, t * LANES:(t + 1) * LANES], 0.0) * wh for t in range(n_lt)]
        score = jnp.concatenate(tiles, axis=1)
        k_pos = off + lax.broadcasted_iota(I32, (tq, ck), 1)
        key_ref[:, pl.ds(off, ck)] = jnp.where(k_pos < limit, _float_key(score), INT_MIN)
        return 0

    lax.fori_loop(0, n_chunks, score_chunk, 0)

    def count_ge(cand):
        def body(c, cnt):
            off = pl.multiple_of(c * ck, ck)
            kc = key_ref[:, pl.ds(off, ck)]
            for t in range(ck // LANES):
                cnt = cnt + jnp.where(kc[:, t * LANES:(t + 1) * LANES] >= cand, 1, 0)
            return cnt
        cnt = lax.fori_loop(0, n_chunks, body, jnp.zeros((tq, LANES), I32))
        return jnp.sum(cnt, axis=1, keepdims=True)

    def bit_step(i, lo):
        cand = lo + lax.shift_left(jnp.int32(1), 31 - i)
        return jnp.where(count_ge(cand) >= top_k, cand, lo)

    lo = lax.fori_loop(0, 32, bit_step, jnp.full((tq, LANES), INT_MIN, I32))
    thr = jnp.maximum(lo, INT_MIN + 1)

    o_ref[0] = jnp.full((tq, s_pad), NEG_INF, o_ref.dtype)

    def write_chunk(c, _):
        off = pl.multiple_of(c * ck, ck)
        kc = key_ref[:, pl.ds(off, ck)]
        o_ref[0, :, pl.ds(off, ck)] = jnp.where(kc >= thr[:, :1], 0.0, NEG_INF).astype(o_ref.dtype)
        return 0

    lax.fori_loop(0, n_chunks, write_chunk, 0)


def _index_call(iq, iw, ik, q_pos0, s_valid, tq, ck):
    bsz, t_q, _ = iq.shape
    s_pad = ik.shape[1]
    top_k = min(IDX_TOPK, s_valid // 4)
    return pl.pallas_call(
        functools.partial(_index_kernel, q_pos0=q_pos0, s_valid=s_valid, tq=tq, ck=ck, top_k=top_k),
        grid=(bsz, t_q // tq),
        in_specs=[
            pl.BlockSpec((1, tq, IDX_WIDTH), lambda b, i: (b, i, 0)),
            pl.BlockSpec((1, tq, LANES), lambda b, i: (b, i, 0)),
            pl.BlockSpec((1, s_pad, IDX_DIM), lambda b, i: (b, 0, 0)),
        ],
        out_specs=pl.BlockSpec((1, tq, s_pad), lambda b, i: (b, i, 0)),
        out_shape=jax.ShapeDtypeStruct((bsz, t_q, s_pad), BF16),
        scratch_shapes=[pltpu.VMEM((tq, s_pad), I32), pltpu.VMEM((IDX_HEADS, tq, LANES), F32)],
        compiler_params=_cparams(("arbitrary", "arbitrary")),
        name="indexer_topk_mask",
    )(iq, iw, ik)


def _diff_kernel(q_ref, k_ref, v_ref, dl_ref, g_ref, o_ref, m_ref, acc_ref, *, q_pos0, s_valid, tq, tk, lam_init):
    iq, ik = pl.program_id(2), pl.program_id(3)
    nk = pl.num_programs(3)
    c = float(C_QK_DIM ** -0.5) * LOG2E

    @pl.when(ik == 0)
    def _():
        m_ref[...] = jnp.full_like(m_ref, NEG_INF)
        acc_ref[...] = jnp.zeros_like(acc_ref)

    last = _last_block(iq, "chunk", q_pos0, tq, tk, s_valid)
    first_masked = (q_pos0 + iq * tq) // tk

    def step(masked):
        if masked:
            q_pos = q_pos0 + iq * tq + lax.broadcasted_iota(I32, (tq, 1), 0)
            k_pos = ik * tk + lax.broadcasted_iota(I32, (tq, tk), 1)
            visible = k_pos < _chunk_limit(q_pos, s_valid)
        for hh in range(HEADS_PER_STEP):
            sl = slice(hh * LANES, (hh + 1) * LANES)
            q = q_ref[0, :, sl]
            k = k_ref[0, :, sl]
            v_ext = _with_ones(v_ref[0, :, sl])
            lane = lax.broadcasted_iota(I32, q.shape, 1)
            for mm, sel in enumerate((lane < C_QK_DIM, lane >= C_QK_DIM)):
                s = _dot_nt(jnp.where(sel, q, jnp.zeros_like(q)), k) * c
                if masked:
                    s = jnp.where(visible, s, NEG_INF)
                _online_softmax_step(s, v_ext, m_ref, acc_ref, 2 * hh + mm)

    @pl.when(ik < jnp.minimum(first_masked, last + 1))
    def _():
        step(False)

    @pl.when((ik >= first_masked) & (ik <= last))
    def _():
        step(True)

    @pl.when(ik == nk - 1)
    def _():
        dl = dl_ref[...]
        lam = (jnp.exp(jnp.sum(dl[0:1] * dl[1:2], axis=1, keepdims=True))
               - jnp.exp(jnp.sum(dl[2:3] * dl[3:4], axis=1, keepdims=True)) + lam_init)
        outs = []
        for hh in range(HEADS_PER_STEP):
            o = _softmax_result(acc_ref, 2 * hh) - lam * _softmax_result(acc_ref, 2 * hh + 1)
            o = o * lax.rsqrt(jnp.mean(o * o, axis=1, keepdims=True) + SUBLN_EPS)
            outs.append(o * g_ref[...] * (1.0 - lam_init))
        o_ref[0] = jnp.concatenate(outs, axis=1).astype(o_ref.dtype)


def _diff_call(q, k, v, dl, g, q_pos0, s_valid, tq, tk, lam_init):
    bsz, t_q, _ = q.shape
    s_pad = k.shape[1]
    hps = HEADS_PER_STEP
    assert C_HEADS % hps == 0
    last = functools.partial(_last_block, mode="chunk", q_pos0=q_pos0, tq=tq, tk=tk, s_valid=s_valid)

    def kv_map(b, hp, iq, ik):
        return (b, jnp.minimum(ik, last(iq)), hp)

    return pl.pallas_call(
        functools.partial(_diff_kernel, q_pos0=q_pos0, s_valid=s_valid, tq=tq, tk=tk, lam_init=lam_init),
        grid=(bsz, C_HEADS // hps, t_q // tq, s_pad // tk),
        in_specs=[
            pl.BlockSpec((1, tq, hps * LANES), lambda b, hp, iq, ik: (b, iq, hp)),
            pl.BlockSpec((1, tk, hps * LANES), kv_map),
            pl.BlockSpec((1, tk, hps * LANES), kv_map),
            pl.BlockSpec((4, C_QK_DIM), lambda b, hp, iq, ik: (0, 0)),
            pl.BlockSpec((1, LANES), lambda b, hp, iq, ik: (0, 0)),
        ],
        out_specs=pl.BlockSpec((1, tq, hps * LANES), lambda b, hp, iq, ik: (b, iq, hp)),
        out_shape=jax.ShapeDtypeStruct((bsz, t_q, C_WIDTH), BF16),
        scratch_shapes=[pltpu.VMEM((2 * hps, tq, LANES), F32), pltpu.VMEM((2 * hps, tq, 2 * LANES), F32)],
        compiler_params=_cparams(("arbitrary",) * 4),
        name="diff_attention",
    )(q, k, v, dl, g)


def _out_kernel(x_ref, oa_ref, ob_ref, oc_ref, w_ref, gt_ref, sc_ref, sh_ref, g_ref, wr_ref, br_ref,
                x1_ref, h_ref, ti_ref, tw_ref):
    attn = (_dot(oa_ref[...], w_ref[0:A_WIDTH, :])
            + _dot(ob_ref[...], w_ref[A_WIDTH:A_WIDTH + B_WIDTH, :])
            + _dot(oc_ref[...], w_ref[A_WIDTH + B_WIDTH:, :]))
    x1 = x_ref[...] + gt_ref[0] * attn
    x1_ref[...] = x1
    h = _rms_mod(x1, g_ref[...], sc_ref[0], sh_ref[0])
    _tok_store(h_ref, h, h.shape[1] // LANES)
    logits = _dot(h, wr_ref[...], precision=lax.Precision.HIGHEST) + br_ref[...]
    lane = lax.broadcasted_iota(I32, logits.shape, 1)
    vals, idxs = [], []
    for _ in range(TOP_K):
        mx = jnp.max(logits, axis=1, keepdims=True)
        ix = jnp.min(jnp.where(logits == mx, lane, LANES), axis=1, keepdims=True)
        vals.append(mx)
        idxs.append(ix)
        logits = jnp.where(lane == ix, -jnp.inf, logits)
    es = [jnp.exp(v - vals[0]) for v in vals]
    den = es[0] + es[1] + es[2] + es[3]
    ti = jnp.zeros(lane.shape, I32)
    tw = jnp.zeros(lane.shape, F32)
    for kk in range(TOP_K):
        ti = jnp.where(lane == kk, idxs[kk], ti)
        tw = jnp.where(lane == kk, es[kk] / den, tw)
    ti_ref[...] = ti
    tw_ref[...] = tw


def _out_call(x2d, oa, ob, oc, w_out, gt, sc, sh, g, w_router, b_router, tm, tiles_per_mod):
    n, d = x2d.shape
    r = gt.shape[1]
    row = lambda w: pl.BlockSpec((tm, w), lambda i: (i, 0))
    mod_spec = pl.BlockSpec((1, r, d), lambda i: (i // tiles_per_mod, 0, 0))
    full = lambda a: pl.BlockSpec(a.shape, lambda i: (0,) * a.ndim)
    return pl.pallas_call(
        _out_kernel,
        grid=(n // tm,),
        in_specs=[row(d), row(A_WIDTH), row(B_WIDTH), row(C_WIDTH), full(w_out), mod_spec, mod_spec, mod_spec,
                  full(g), full(w_router), full(b_router)],
        out_specs=[row(d), pl.BlockSpec((tm * (d // LANES), LANES), lambda i: (i, 0)), row(LANES), row(LANES)],
        out_shape=[jax.ShapeDtypeStruct((n, d), F32), jax.ShapeDtypeStruct((n * (d // LANES), LANES), F32),
                   jax.ShapeDtypeStruct((n, LANES), I32), jax.ShapeDtypeStruct((n, LANES), F32)],
        compiler_params=_cparams(("arbitrary",)),
        name="out_proj_router",
    )(x2d, oa, ob, oc, w_out, gt, sc, sh, g, w_router, b_router)


def _tok_load(ref, a, n_tok, r):
    return ref[pl.ds(a, n_tok, stride=r), :]


def _tok_store(ref, x, r):
    n_tok = x.shape[0]
    for a in range(r):
        ref[pl.ds(a, n_tok, stride=r), :] = x[:, a * LANES:(a + 1) * LANES]


def _row_gather(src_hbm, idx_smem, dst_ref, sem, n_rows, r):
    def copy(i):
        src = pl.multiple_of(idx_smem[i] * r, r)
        dst = pl.multiple_of(i * r, r)
        return pltpu.make_async_copy(src_hbm.at[pl.ds(src, r)], dst_ref.at[pl.ds(dst, r)], sem)

    def start(r, _):
        copy(r).start()
        return 0

    def wait(r, _):
        copy(r).wait()
        return 0

    lax.fori_loop(0, n_rows, start, 0)
    lax.fori_loop(0, n_rows, wait, 0)


def _gather_kernel(idx_hbm, src_hbm, o_ref, idx_smem, sem_idx, sem_rows, *, r):
    i = pl.program_id(0)
    cp = pltpu.make_async_copy(idx_hbm.at[i], idx_smem, sem_idx)
    cp.start()
    cp.wait()
    _row_gather(src_hbm, idx_smem, o_ref, sem_rows, o_ref.shape[0] // r, r)


def _gather_call(src, idx, tr, r):
    p = idx.shape[0]
    return pl.pallas_call(
        functools.partial(_gather_kernel, r=r),
        grid=(p // tr,),
        in_specs=[pl.BlockSpec(memory_space=pl.ANY), pl.BlockSpec(memory_space=pl.ANY)],
        out_specs=pl.BlockSpec((tr * r, LANES), lambda i: (i, 0)),
        out_shape=jax.ShapeDtypeStruct((p * r, LANES), src.dtype),
        scratch_shapes=[pltpu.SMEM((tr,), I32), pltpu.SemaphoreType.DMA, pltpu.SemaphoreType.DMA],
        compiler_params=_cparams(("arbitrary",)),
        name="moe_row_gather",
    )(idx.reshape(p // tr, tr), src)


def _expert_kernel(te_ref, nt_ref, x_ref, wg_ref, wu_ref, bg_ref, bu_ref, wd_ref, bd_ref, o_ref, xb_ref, acc_ref):
    t, j = pl.program_id(0), pl.program_id(1)
    nf = pl.num_programs(1)
    used = t < nt_ref[0]
    tm, d = xb_ref.shape
    r = d // LANES

    @pl.when(used & (j == 0))
    def _():
        for a in range(r):
            xb_ref[:, a * LANES:(a + 1) * LANES] = _tok_load(x_ref, a, tm, r).astype(BF16)
        acc_ref[...] = jnp.zeros_like(acc_ref)

    @pl.when(used)
    def _():
        x = xb_ref[...]
        gate = jnp.minimum(_dot(x, wg_ref[0]) + bg_ref[0], SWIGLU_LIMIT)
        up = jnp.clip(_dot(x, wu_ref[0]) + bu_ref[0], -SWIGLU_LIMIT, SWIGLU_LIMIT)
        act = (up + 1.0) * (gate * jax.nn.sigmoid(gate * SWIGLU_ALPHA))
        acc_ref[...] += _dot(act.astype(BF16), wd_ref[0])

    @pl.when(used & (j == nf - 1))
    def _():
        _tok_store(o_ref, acc_ref[...] + bd_ref[0], r)

    @pl.when(jnp.logical_not(used) & (j == nf - 1))
    def _():
        o_ref[...] = jnp.zeros_like(o_ref)


def _expert_call(xs, tile_expert, n_tiles_used, w_gu, b_gu, w_d, b_d, tm, tf):
    n_exp, d, f2 = w_gu.shape
    r = d // LANES
    p = xs.shape[0] // r
    f = f2 // 2
    nf = f // tf
    grid_spec = pltpu.PrefetchScalarGridSpec(
        num_scalar_prefetch=2,
        grid=(p // tm, nf),
        in_specs=[
            pl.BlockSpec((tm * r, LANES), lambda t, j, te, nt: (t, 0)),
            pl.BlockSpec((1, d, tf), lambda t, j, te, nt: (te[t], 0, j)),
            pl.BlockSpec((1, d, tf), lambda t, j, te, nt: (te[t], 0, nf + j)),
            pl.BlockSpec((1, 1, tf), lambda t, j, te, nt: (te[t], 0, j)),
            pl.BlockSpec((1, 1, tf), lambda t, j, te, nt: (te[t], 0, nf + j)),
            pl.BlockSpec((1, tf, d), lambda t, j, te, nt: (te[t], j, 0)),
            pl.BlockSpec((1, 1, d), lambda t, j, te, nt: (te[t], 0, 0)),
        ],
        out_specs=pl.BlockSpec((tm * r, LANES), lambda t, j, te, nt: (t, 0)),
        scratch_shapes=[pltpu.VMEM((tm, d), BF16), pltpu.VMEM((tm, d), F32)],
    )
    return pl.pallas_call(
        _expert_kernel,
        grid_spec=grid_spec,
        out_shape=jax.ShapeDtypeStruct((p * r, LANES), F32),
        compiler_params=_cparams(("arbitrary", "arbitrary")),
        name="moe_experts",
    )(tile_expert, n_tiles_used, xs, w_gu, w_gu, b_gu.reshape(n_exp, 1, f2), b_gu.reshape(n_exp, 1, f2),
      w_d, b_d.reshape(n_exp, 1, d))


def _combine_kernel(pos_hbm, y_hbm, x_ref, tw_ref, gt_ref, gf_ref, o_ref, idx_smem, rows_ref, sem_idx, sem_rows,
                    *, final_norm):
    i = pl.program_id(0)
    tn = x_ref.shape[0]
    cp = pltpu.make_async_copy(pos_hbm.at[i], idx_smem, sem_idx)
    cp.start()
    cp.wait()
    r = x_ref.shape[1] // LANES
    _row_gather(y_hbm, idx_smem, rows_ref, sem_rows, TOP_K * tn, r)
    tw = tw_ref[...]
    w_rep = [jnp.broadcast_to(tw[:, kk:kk + 1], (tn, LANES)) for kk in range(TOP_K)]
    tiles = []
    for a in range(r):
        acc = jnp.zeros((tn, LANES), F32)
        for kk in range(TOP_K):
            acc = acc + w_rep[kk] * rows_ref[pl.ds(kk * tn * r + a, tn, stride=r), :]
        tiles.append(acc)
    moe = jnp.concatenate(tiles, axis=1)
    x2 = x_ref[...] + gt_ref[0] * moe
    if final_norm:
        x2 = x2 * lax.rsqrt(jnp.mean(x2 * x2, axis=-1, keepdims=True) + RMS_EPS) * gf_ref[...]
    o_ref[...] = x2


def _combine_call(pos, y_sorted, x1, top_w, gt, tiles_per_mod, g_final, tn, final_norm):
    n, d = x1.shape
    r = gt.shape[1]
    pos_tiles = pos.reshape(n // tn, tn, TOP_K).transpose(0, 2, 1).reshape(n // tn, TOP_K * tn)
    return pl.pallas_call(
        functools.partial(_combine_kernel, final_norm=final_norm),
        grid=(n // tn,),
        in_specs=[
            pl.BlockSpec(memory_space=pl.ANY), pl.BlockSpec(memory_space=pl.ANY),
            pl.BlockSpec((tn, d), lambda i: (i, 0)),
            pl.BlockSpec((tn, LANES), lambda i: (i, 0)),
            pl.BlockSpec((1, r, d), lambda i: (i // tiles_per_mod, 0, 0)),
            pl.BlockSpec((1, d), lambda i: (0, 0)),
        ],
        out_specs=pl.BlockSpec((tn, d), lambda i: (i, 0)),
        out_shape=jax.ShapeDtypeStruct((n, d), F32),
        scratch_shapes=[pltpu.SMEM((TOP_K * tn,), I32), pltpu.VMEM((TOP_K * tn * (d // LANES), LANES), F32),
                        pltpu.SemaphoreType.DMA, pltpu.SemaphoreType.DMA],
        compiler_params=_cparams(("arbitrary",)),
        name="moe_combine",
    )(pos_tiles, y_sorted, x1, top_w, gt, g_final)


def _routing_plan(top_i, n_exp, tm):
    n = top_i.shape[0]
    m = n * TOP_K
    flat = top_i.reshape(-1)
    order = jnp.argsort(flat, stable=True).astype(I32)
    inv = jnp.argsort(order).astype(I32)
    sorted_e = flat[order]
    experts = jnp.arange(n_exp, dtype=I32)
    end = jnp.sum(sorted_e[None, :] <= experts[:, None], axis=1).astype(I32)
    start = end - jnp.diff(end, prepend=0)
    counts = end - start
    padded = (counts + tm - 1) // tm * tm
    pend = jnp.cumsum(padded)
    pstart = pend - padded
    dest = pstart[sorted_e] + jnp.arange(m, dtype=I32) - start[sorted_e]
    pos = dest[inv].reshape(n, TOP_K)
    p = (m + n_exp * (tm - 1)) // tm * tm
    rows = jnp.arange(p, dtype=I32)
    e_row = jnp.minimum(jnp.sum(rows[:, None] >= pend[None, :], axis=1), n_exp - 1).astype(I32)
    rank = rows - pstart[e_row]
    src_sorted = jnp.clip(start[e_row] + rank, 0, m - 1)
    src_row = jnp.where(rank < counts[e_row], order[src_sorted] // TOP_K, 0)
    tile_expert = e_row[::tm]
    n_tiles_used = (pend[-1] // tm).astype(I32).reshape(1)
    return src_row, pos, tile_expert, n_tiles_used


def _pad_rows(a, axis, size):
    pad = size - a.shape[axis]
    if pad == 0:
        return a
    widths = [(0, 0)] * a.ndim
    widths[axis] = (0, pad)
    return jnp.pad(a, widths)


def _mixers(pr, new_logf, past, q_pos0, dl, g_diff, lam_init, tq, tk, tq_idx, ck_idx):
    bsz, t = pr["fq_b"].shape[:2]
    keys = {}
    if past is None:
        for nm in ("fk", "fv", "dk", "dv", "ik", "ck", "cv"):
            keys[nm] = pr[nm + "_b"]
        logf_all = new_logf
        s_valid = t
    else:
        s_valid = past["fk"].shape[1] + t
        s_pad = -(-s_valid // ck_idx) * ck_idx
        tk = s_pad
        for nm in ("fk", "fv", "dk", "dv", "ik", "ck", "cv"):
            cache = past[nm].reshape(bsz, past[nm].shape[1], -1).astype(BF16)
            keys[nm] = _pad_rows(jnp.concatenate([cache, pr[nm + "_b"]], axis=1), 1, s_pad)
        logf_all = jnp.concatenate([past["logf"], new_logf], axis=1)
    s_pad = keys["fk"].shape[1]

    chan = jnp.transpose(logf_all, (1, 0, 2)).reshape(logf_all.shape[1], bsz * A_HEADS)
    cum = _cumsum_call(_pad_rows(_pad_rows(chan, 1, LANES), 0, s_pad))
    o_a = _flash_call("fox", pr["fq_b"], keys["fk"], keys["fv"], (cum, cum.T), A_HEADS, q_pos0, s_valid, tq, tk)

    bias = _index_call(pr["iq_b"], pr["small"], keys["ik"], q_pos0, s_valid, tq_idx, ck_idx)
    o_b = _flash_call("dsa", pr["dq_b"], keys["dk"], keys["dv"], (bias,), B_HEADS, q_pos0, s_valid, tq, tk)

    o_c = _diff_call(pr["cq_b"], keys["ck"], keys["cv"], dl, g_diff, q_pos0, s_valid, tq, tk, lam_init)
    return o_a, o_b, o_c


def kernel(x_prompt, x_sample, cache_fox_k, cache_fox_v, cache_fox_logf, cache_dsa_k, cache_dsa_v,
           cache_dsa_idx_k, cache_diff_k, cache_diff_v, c_prompt, c_sample, w_ada, b_ada, g_attn, g_ffn,
           w_in, b_forget, diff_lambda, g_diff, w_out, w_router, b_router, w_gate_up, b_gate_up,
           w_down, b_down, g_final):
    depth = w_ada.shape[0]
    bp, tp, d = x_prompt.shape
    bs, ts, _ = x_sample.shape
    past_len = cache_fox_k.shape[2]
    n_exp = w_router.shape[-1]
    np_, ns = bp * tp, bs * ts

    tm_p = min(512, tp)
    tm_o = min(512, tp)
    tq = min(512, tp)
    tk = min(1024, tp)
    tq_idx = min(128, tp)
    ck_idx = min(512, tp)
    tm_e = 512
    tn_c = min(256, ns)

    m_rows = -(-(bp + bs) // 16) * 16
    c_all = _pad_rows(jnp.concatenate([c_prompt, c_sample], axis=0), 0, m_rows)
    mod = _ada_call(c_all, w_ada, b_ada)

    tabs_p = _rope_tables(jnp.arange(tp, dtype=I32))
    tabs_s = [jnp.tile(tb, (bs, 1)) for tb in _rope_tables(jnp.arange(past_len, past_len + ts, dtype=I32))]

    offs = {}
    acc = 0
    for nm, wdt in (("fq", A_WIDTH), ("fk", A_WIDTH), ("fv", A_WIDTH), ("ff", A_HEADS), ("dq", B_WIDTH),
                    ("dk", B_WIDTH), ("dv", B_WIDTH), ("iq", IDX_WIDTH), ("ik", IDX_DIM), ("iw", IDX_HEADS),
                    ("cq", C_WIDTH), ("ck", C_WIDTH), ("cv", C_WIDTH)):
        offs[nm] = (acc, acc + wdt)
        acc += wdt
    col_order = ("fq", "fk", "fv", "dq", "dk", "dv", "iq", "cq", "ck", "cv", "ik", "ff", "iw")

    xp = x_prompt.reshape(np_, d)
    xs = x_sample.reshape(ns, d)
    rows_p, rows_s = [], []
    for l in range(depth):
        lam_init = 0.8 - 0.6 * math.exp(-0.3 * l)
        w_cat = jnp.concatenate([w_in[l][:, offs[nm][0]:offs[nm][1]] for nm in col_order], axis=1)
        w_cat = _pad_rows(w_cat, 1, _PROJ_TILES * PROJ_TN).astype(BF16)
        b_small = _pad_rows(b_forget[l].reshape(1, A_HEADS), 1, LANES)
        w_out_b = w_out[l].astype(BF16)
        w_r = _pad_rows(w_router[l], 1, LANES)
        b_r = jnp.concatenate([b_router[l], jnp.full((LANES - n_exp,), NEG_INF, F32)]).reshape(1, LANES)
        w_gu_b = w_gate_up[l].astype(BF16)
        w_d_b = w_down[l].astype(BF16)

        mods_p = [m.reshape(bp, 1, d) for m in jnp.split(mod[l, :bp], 6, axis=-1)]
        mods_s = [jnp.broadcast_to(m[:, None, :], (bs, ts, d)).reshape(ns, d)
                  for m in jnp.split(mod[l, bp:bp + bs], 6, axis=-1)]

        def mod_view(m, t, tile):
            if m.ndim == 3:
                return m, t // tile
            return m.reshape(m.shape[0] // tile, tile, d), 1

        groups = []
        for (x2d, mods, tabs, bsz, t, tm, past, q_pos0, tqg, tqi) in (
                (xp, mods_p, tabs_p, bp, tp, tm_p, None, 0, tq, tq_idx),
                (xs, mods_s, tabs_s, bs, ts, ns,
                 dict(fk=cache_fox_k[l], fv=cache_fox_v[l], logf=cache_fox_logf[l], dk=cache_dsa_k[l],
                      dv=cache_dsa_v[l], ik=cache_dsa_idx_k[l], ck=cache_diff_k[l], cv=cache_diff_v[l]),
                 past_len, ts, ts)):
            sh1, sc1, gt1, sh2, sc2, gt2 = mods
            (sc1v, tpm), (sh1v, _) = mod_view(sc1, t, tm), mod_view(sh1, t, tm)
            pr = _proj_call(x2d, sc1v, sh1v, g_attn[l].reshape(1, d), w_cat, tabs, b_small, tm, tpm)
            pr3 = {nm: a.reshape(bsz, t, a.shape[-1]) for nm, a in pr.items()}
            new_logf = pr3["small"][:, :, :A_HEADS]
            o_a, o_b, o_c = _mixers(pr3, new_logf, past, q_pos0, diff_lambda[l], g_diff[l].reshape(1, LANES),
                                    lam_init, tqg, tk, tqi, ck_idx)
            tmo = min(tm_o, bsz * t)
            (gt1v, tpm_o), (sc2v, _), (sh2v, _) = (mod_view(m, t, tmo) for m in (gt1, sc2, sh2))
            x1, h2, top_i, top_w = _out_call(
                x2d, o_a.reshape(bsz * t, -1), o_b.reshape(bsz * t, -1), o_c.reshape(bsz * t, -1), w_out_b,
                gt1v, sc2v, sh2v, g_ffn[l].reshape(1, d), w_r, b_r, tmo, tpm_o)
            rows = (pr3["fk_f"].reshape(bsz, t, A_HEADS, HEAD_DIM), pr3["fv_f"].reshape(bsz, t, A_HEADS, HEAD_DIM),
                    new_logf, pr3["dk_f"].reshape(bsz, t, B_HEADS, HEAD_DIM),
                    pr3["dv_f"].reshape(bsz, t, B_HEADS, HEAD_DIM), pr3["ik_f"],
                    pr3["ck_f"].reshape(bsz, t, C_HEADS, 2, C_QK_DIM), pr3["cv_f"].reshape(bsz, t, C_HEADS, 2 * C_QK_DIM))
            groups.append((x1, h2, top_i, top_w, gt2, rows))

        (x1p, h2p, tip, twp, gt2p, rp), (x1s, h2s, tis, tws, gt2s, rs) = groups
        rows_p.append(rp)
        rows_s.append(rs)

        h_all = jnp.concatenate([h2p, h2s], axis=0)
        ti_all = jnp.concatenate([tip[:, :TOP_K], tis[:, :TOP_K]], axis=0)
        src_row, pos, tile_expert, n_tiles_used = _routing_plan(ti_all, n_exp, tm_e)
        x_sorted = _gather_call(h_all, src_row, tm_e, d // LANES)
        y_sorted = _expert_call(x_sorted, tile_expert, n_tiles_used, w_gu_b, b_gate_up[l], w_d_b, b_down[l],
                                tm_e, min(512, w_d_b.shape[1]))
        final = l == depth - 1
        gf = g_final.reshape(1, d)
        xp = _combine_call(pos[:np_], y_sorted, x1p, twp, *mod_view(gt2p, tp, tn_c), gf, tn_c, final)
        xs = _combine_call(pos[np_:], y_sorted, x1s, tws, *mod_view(gt2s, ts, tn_c), gf, tn_c, final)

    def stack(rows, i):
        return jnp.stack([r[i] for r in rows], axis=0)

    return ((xp.reshape(bp, tp, d), xs.reshape(bs, ts, d))
            + tuple(stack(rows_p, i) for i in range(8)) + tuple(stack(rows_s, i) for i in range(8)))
```

```python
import functools
import math

import jax
import jax.numpy as jnp
from jax import lax
from jax.experimental import pallas as pl
from jax.experimental.pallas import tpu as pltpu

F32 = jnp.float32
BF16 = jnp.bfloat16
I32 = jnp.int32

LANES = 128
HEAD_DIM = 128
A_HEADS = 6
B_HEADS = 6
C_HEADS = 4
C_QK_DIM = 64
IDX_HEADS = 16
IDX_DIM = 128
IDX_TOPK = 256
CHUNK = 64
ROPE_THETA = 500000.0
ROPE_FRACTION = 4
TOP_K = 4
SWIGLU_LIMIT = 7.0
SWIGLU_ALPHA = 1.702
RMS_EPS = 1e-6
SUBLN_EPS = 1e-5
NEG_INF = -1e30
INT_MIN = -2 ** 31
LOG2E = 1.4426950408889634
HEADS_PER_STEP = 2

A_WIDTH = A_HEADS * HEAD_DIM
B_WIDTH = B_HEADS * HEAD_DIM
C_WIDTH = C_HEADS * 2 * C_QK_DIM
IDX_WIDTH = IDX_HEADS * IDX_DIM
N_SMALL_FORGET = A_HEADS
N_SMALL_IW = IDX_HEADS

PROJ_TN = 256
VMEM_LIMIT = 56 * 1024 * 1024


def _cparams(sem):
    return pltpu.CompilerParams(dimension_semantics=sem, vmem_limit_bytes=VMEM_LIMIT)


def _dot_nt(a, b):
    return lax.dot_general(a, b, (((1,), (1,)), ((), ())), preferred_element_type=F32)


def _dot(a, b, precision=None):
    return jnp.dot(a, b, preferred_element_type=F32, precision=precision)


def _ada_kernel(c_ref, w_ref, b_ref, o_ref):
    c = c_ref[...]
    a = (c * jax.nn.sigmoid(c)).astype(BF16)
    o_ref[0] = _dot(a, w_ref[0].astype(BF16)) + b_ref[0]


def _ada_call(c_all, w_ada, b_ada):
    depth, d, n = w_ada.shape
    m = c_all.shape[0]
    tn = 512
    return pl.pallas_call(
        _ada_kernel,
        grid=(depth, n // tn),
        in_specs=[
            pl.BlockSpec((m, d), lambda l, j: (0, 0)),
            pl.BlockSpec((1, d, tn), lambda l, j: (l, 0, j)),
            pl.BlockSpec((1, 1, tn), lambda l, j: (l, 0, j)),
        ],
        out_specs=pl.BlockSpec((1, m, tn), lambda l, j: (l, 0, j)),
        out_shape=jax.ShapeDtypeStruct((depth, m, n), F32),
        compiler_params=_cparams(("arbitrary", "arbitrary")),
        name="ada_mod",
    )(c_all, w_ada, b_ada.reshape(depth, 1, n))


def _cumsum_kernel(x_ref, o_ref, carry_ref):
    @pl.when(pl.program_id(0) == 0)
    def _():
        carry_ref[...] = jnp.zeros_like(carry_ref)

    tb = x_ref.shape[0]
    row = lax.broadcasted_iota(I32, (tb, tb), 0)
    col = lax.broadcasted_iota(I32, (tb, tb), 1)
    tri = jnp.where(col <= row, 1.0, 0.0).astype(F32)
    cs = _dot(tri, x_ref[...], precision=lax.Precision.HIGHEST) + carry_ref[0:1, :]
    o_ref[...] = cs
    carry_ref[...] = jnp.broadcast_to(cs[tb - 1:tb, :], carry_ref.shape)


def _cumsum_call(x):
    s = x.shape[0]
    tb = 256
    return pl.pallas_call(
        _cumsum_kernel,
        grid=(s // tb,),
        in_specs=[pl.BlockSpec((tb, LANES), lambda i: (i, 0))],
        out_specs=pl.BlockSpec((tb, LANES), lambda i: (i, 0)),
        out_shape=jax.ShapeDtypeStruct((s, LANES), F32),
        scratch_shapes=[pltpu.VMEM((8, LANES), F32)],
        compiler_params=_cparams(("arbitrary",)),
        name="forget_cumsum",
    )(x)


_PROJ_PIECES = (
    ("fq", 0, 3, None, False, True),
    ("fk", 3, 3, None, True, True),
    ("fv", 6, 3, None, True, True),
    ("dq", 9, 3, "r128", False, True),
    ("dk", 12, 3, "r128", True, True),
    ("dv", 15, 3, None, True, True),
    ("iq", 18, 8, "r128", False, True),
    ("cq", 26, 2, "r64", False, True),
    ("ck", 28, 2, "r64", True, True),
    ("cv", 30, 2, None, True, True),
)
_PROJ_LAST_TILE = 32
_PROJ_TILES = 33


def _rms_mod(x, g, sc, sh):
    y = x * lax.rsqrt(jnp.mean(x * x, axis=-1, keepdims=True) + RMS_EPS)
    return (y * g) * (1.0 + sc) + sh


def _rope_tile(z, c, s1, s2, shift):
    return z * c + pltpu.roll(z, LANES - shift, 1) * s1 + pltpu.roll(z, shift, 1) * s2


def _proj_kernel(x_ref, sc_ref, sh_ref, g_ref, w_ref, c128, s1128, s2128, c64, s164, s264, bs_ref,
                 *rest, iw_scale):
    outs = rest[:-1]
    h_s = rest[-1]
    j = pl.program_id(1)

    @pl.when(j == 0)
    def _():
        h_s[...] = _rms_mod(x_ref[...], g_ref[...], sc_ref[0], sh_ref[0]).astype(BF16)

    z = _dot(h_s[...], w_ref[...])

    def rope(zz, kind):
        if kind is None:
            return zz
        tabs, shift = ((c128, s1128, s2128), 16) if kind == "r128" else ((c64, s164, s264), 8)
        c, s1, s2 = (t[...] for t in tabs)
        return jnp.concatenate(
            [_rope_tile(zz[:, t * LANES:(t + 1) * LANES], c, s1, s2, shift) for t in range(2)], axis=1)

    k = 0
    for (_, start, cnt, kind, wf, wb) in _PROJ_PIECES:
        refs = outs[k:k + int(wf) + int(wb)]
        k += int(wf) + int(wb)

        @pl.when((j >= start) & (j < start + cnt))
        def _(refs=refs, kind=kind):
            r = rope(z, kind)
            for o in refs:
                o[...] = r.astype(o.dtype)

    ik_f, ik_b, small = outs[k:k + 3]

    @pl.when(j == _PROJ_LAST_TILE)
    def _():
        r = _rope_tile(z[:, :LANES], c128[...], s1128[...], s2128[...], 16)
        ik_f[...] = r
        ik_b[...] = r.astype(BF16)
        zs = z[:, LANES:]
        lane = lax.broadcasted_iota(I32, zs.shape, 1)
        zf = zs + bs_ref[...]
        logsig = jnp.minimum(zf, 0.0) - jnp.log(1.0 + jnp.exp(-jnp.abs(zf)))
        small[...] = jnp.where(lane < N_SMALL_FORGET, logsig,
                               jnp.where(lane < N_SMALL_FORGET + N_SMALL_IW, zs * iw_scale, zs))


def _proj_call(x2d, sc, sh, g, w_cat, tabs, b_small, tm, tiles_per_mod):
    n, d = x2d.shape
    r = sc.shape[1]
    ntab = tabs[0].shape[0] // tm
    out_shapes, out_specs = [], []

    def piece_spec(start, cnt, width):
        return pl.BlockSpec((tm, width), lambda i, j: (i, jnp.clip(j - start, 0, cnt - 1)))

    for (_, start, cnt, _, wf, wb) in _PROJ_PIECES:
        for dt, on in ((F32, wf), (BF16, wb)):
            if on:
                out_shapes.append(jax.ShapeDtypeStruct((n, cnt * PROJ_TN), dt))
                out_specs.append(piece_spec(start, cnt, PROJ_TN))
    for dt in (F32, BF16, F32):
        out_shapes.append(jax.ShapeDtypeStruct((n, LANES), dt))
        out_specs.append(pl.BlockSpec((tm, LANES), lambda i, j: (i, 0)))

    mod_spec = pl.BlockSpec((1, r, d), lambda i, j: (i // tiles_per_mod, 0, 0))
    tab_spec = pl.BlockSpec((tm, LANES), lambda i, j: (i % ntab, 0))
    outs = pl.pallas_call(
        functools.partial(_proj_kernel, iw_scale=float(IDX_HEADS ** -0.5 * IDX_DIM ** -0.5)),
        grid=(n // tm, _PROJ_TILES),
        in_specs=[
            pl.BlockSpec((tm, d), lambda i, j: (i, 0)),
            mod_spec, mod_spec,
            pl.BlockSpec((1, d), lambda i, j: (0, 0)),
            pl.BlockSpec((d, PROJ_TN), lambda i, j: (0, j)),
            tab_spec, tab_spec, tab_spec, tab_spec, tab_spec, tab_spec,
            pl.BlockSpec((1, LANES), lambda i, j: (0, 0)),
        ],
        out_specs=out_specs,
        out_shape=out_shapes,
        scratch_shapes=[pltpu.VMEM((tm, d), BF16)],
        compiler_params=_cparams(("arbitrary", "arbitrary")),
        name="norm_in_proj",
    )(x2d, sc, sh, g, w_cat, *tabs, b_small)
    names = []
    for (nm, _, _, _, wf, wb) in _PROJ_PIECES:
        if wf:
            names.append(nm + "_f")
        if wb:
            names.append(nm + "_b")
    names += ["ik_f", "ik_b", "small"]
    return dict(zip(names, outs))


def _rope_tables(pos):
    posf = pos.astype(F32)[:, None]
    lane = jnp.arange(LANES)
    tabs = []
    for hd in (HEAD_DIM, C_QK_DIM):
        half = hd // ROPE_FRACTION // 2
        inv_freq = ROPE_THETA ** (-jnp.arange(half, dtype=F32) / half)
        ang = posf * inv_freq[None, :]
        cos, sin = jnp.cos(ang), jnp.sin(ang)
        off = lane % hd
        first = off < half
        second = (off >= half) & (off < 2 * half)
        idx = jnp.where(first, off, jnp.where(second, off - half, 0))
        cos_l, sin_l = cos[:, idx], sin[:, idx]
        tabs.append(jnp.where((first | second)[None, :], cos_l, 1.0))
        tabs.append(jnp.where(first[None, :], -sin_l, 0.0))
        tabs.append(jnp.where(second[None, :], sin_l, 0.0))
    return tabs


def _chunk_limit(q_pos, s_valid):
    return jnp.minimum((q_pos // CHUNK + 1) * CHUNK, s_valid)


def _last_block(iq, mode, q_pos0, tq, tk, s_valid):
    q_last = q_pos0 + (iq + 1) * tq - 1
    if mode == "fox":
        k_last = jnp.minimum(q_last, s_valid - 1)
    else:
        k_last = _chunk_limit(q_last, s_valid) - 1
    return k_last // tk


def _lane_tiles(x):
    return [x[:, t * LANES:(t + 1) * LANES] for t in range(x.shape[1] // LANES)]


def _add_rep(s, rep):
    return jnp.concatenate([st + rep for st in _lane_tiles(s)], axis=1)


def _with_ones(v):
    return jnp.concatenate([v, jnp.ones_like(v)], axis=1)


def _online_softmax_step(s, v_ext, m_ref, acc_ref, i):
    m_prev = m_ref[i]
    m_next = jnp.maximum(m_prev, jnp.max(s, axis=1, keepdims=True))
    alpha = jnp.exp2(m_prev - m_next)
    p = jnp.concatenate([jnp.exp2(st - m_next) for st in _lane_tiles(s)], axis=1)
    acc_ref[i] = acc_ref[i] * jnp.concatenate([alpha, alpha], axis=1) + _dot(p.astype(v_ext.dtype), v_ext)
    m_ref[i] = m_next


def _softmax_result(acc_ref, i):
    acc = acc_ref[i]
    return acc[:, :HEAD_DIM] / acc[:, HEAD_DIM:]


def _flash_kernel(*refs, mode, q_pos0, s_valid, tq, tk, n_heads):
    if mode == "fox":
        q_ref, k_ref, v_ref, cq_ref, ck_ref, o_ref, m_ref, acc_ref, fq_ref = refs
    else:
        q_ref, k_ref, v_ref, bias_ref, o_ref, m_ref, acc_ref = refs
    b, hp, iq, ik = (pl.program_id(a) for a in range(4))
    nk = pl.num_programs(3)
    ch0 = b * n_heads + hp * HEADS_PER_STEP
    c = float(HEAD_DIM ** -0.5) * LOG2E

    @pl.when(ik == 0)
    def _():
        m_ref[...] = jnp.full_like(m_ref, NEG_INF)
        acc_ref[...] = jnp.zeros_like(acc_ref)
        if mode == "fox":
            lane = lax.broadcasted_iota(I32, cq_ref.shape, 1)
            cq = cq_ref[...]
            for hh in range(HEADS_PER_STEP):
                col = jnp.sum(jnp.where(lane == ch0 + hh, cq, 0.0), axis=1, keepdims=True)
                fq_ref[hh] = jnp.broadcast_to(col * LOG2E, (tq, LANES))

    last = _last_block(iq, mode, q_pos0, tq, tk, s_valid)

    def step(masked):
        if masked:
            q_pos = q_pos0 + iq * tq + lax.broadcasted_iota(I32, (tq, tk), 0)
            k_pos = ik * tk + lax.broadcasted_iota(I32, (tq, tk), 1)
            visible = k_pos <= q_pos
        for hh in range(HEADS_PER_STEP):
            sl = slice(hh * HEAD_DIM, (hh + 1) * HEAD_DIM)
            s = _dot_nt(q_ref[0, :, sl], k_ref[0, :, sl]) * c
            if mode == "fox":
                fk = ck_ref[pl.ds(ch0 % 8 + hh, 1), :] * LOG2E
                s = _add_rep(s, fq_ref[hh]) - fk
                if masked:
                    s = jnp.where(visible, s, NEG_INF)
            else:
                s = s + bias_ref[0].astype(F32)
            _online_softmax_step(s, _with_ones(v_ref[0, :, sl]), m_ref, acc_ref, hh)

    if mode == "fox":
        first_masked = (q_pos0 + iq * tq) // tk

        @pl.when(ik < jnp.minimum(first_masked, last + 1))
        def _():
            step(False)

        @pl.when((ik >= first_masked) & (ik <= last))
        def _():
            step(True)
    else:
        @pl.when(ik <= last)
        def _():
            step(False)

    @pl.when(ik == nk - 1)
    def _():
        o_ref[0] = jnp.concatenate([_softmax_result(acc_ref, hh) for hh in range(HEADS_PER_STEP)],
                                   axis=1).astype(o_ref.dtype)


def _flash_call(mode, q, k, v, extra, n_heads, q_pos0, s_valid, tq, tk):
    bsz, t_q, _ = q.shape
    s_pad = k.shape[1]
    nq, nk = t_q // tq, s_pad // tk
    hps = HEADS_PER_STEP
    assert n_heads % hps == 0 and 8 % hps == 0
    last = functools.partial(_last_block, mode=mode, q_pos0=q_pos0, tq=tq, tk=tk, s_valid=s_valid)

    def kv_map(b, hp, iq, ik):
        return (b, jnp.minimum(ik, last(iq)), hp)

    in_specs = [
        pl.BlockSpec((1, tq, hps * HEAD_DIM), lambda b, hp, iq, ik: (b, iq, hp)),
        pl.BlockSpec((1, tk, hps * HEAD_DIM), kv_map),
        pl.BlockSpec((1, tk, hps * HEAD_DIM), kv_map),
    ]
    scratch = [pltpu.VMEM((hps, tq, LANES), F32), pltpu.VMEM((hps, tq, 2 * HEAD_DIM), F32)]
    if mode == "fox":
        cum, cum_t = extra
        qb0 = q_pos0 // tq
        in_specs += [
            pl.BlockSpec((tq, LANES), lambda b, hp, iq, ik: (qb0 + iq, 0)),
            pl.BlockSpec((8, tk), lambda b, hp, iq, ik: ((b * n_heads + hp * hps) // 8, jnp.minimum(ik, last(iq)))),
        ]
        scratch.append(pltpu.VMEM((hps, tq, LANES), F32))
        args = (q, k, v, cum, cum_t)
    else:
        (bias,) = extra
        in_specs.append(pl.BlockSpec((1, tq, tk), lambda b, hp, iq, ik: (b, iq, jnp.minimum(ik, last(iq)))))
        args = (q, k, v, bias)
    return pl.pallas_call(
        functools.partial(_flash_kernel, mode=mode, q_pos0=q_pos0, s_valid=s_valid, tq=tq, tk=tk, n_heads=n_heads),
        grid=(bsz, n_heads // hps, nq, nk),
        in_specs=in_specs,
        out_specs=pl.BlockSpec((1, tq, hps * HEAD_DIM), lambda b, hp, iq, ik: (b, iq, hp)),
        out_shape=jax.ShapeDtypeStruct((bsz, t_q, n_heads * HEAD_DIM), BF16),
        scratch_shapes=scratch,
        compiler_params=_cparams(("arbitrary",) * 4),
        name=mode + "_attention",
    )(*args)


def _float_key(x):
    bits = pltpu.bitcast(x + 0.0, I32)
    return jnp.where(bits < 0, bits ^ jnp.int32(0x7FFFFFFF), bits)


def _index_kernel(iq_ref, iw_ref, ik_ref, o_ref, key_ref, wrep_ref, *, q_pos0, s_valid, tq, ck, top_k):
    iq = pl.program_id(1)
    s_pad = ik_ref.shape[1]
    q_pos = q_pos0 + iq * tq + lax.broadcasted_iota(I32, (tq, 1), 0)
    limit = _chunk_limit(q_pos, s_valid)
    n_adm = _chunk_limit(q_pos0 + (iq + 1) * tq - 1, s_valid)
    n_chunks = (n_adm + ck - 1) // ck

    iw = iw_ref[0]
    for h in range(IDX_HEADS):
        wrep_ref[h] = jnp.broadcast_to(iw[:, N_SMALL_FORGET + h:N_SMALL_FORGET + h + 1], (tq, LANES))
    n_lt = ck // LANES

    def score_chunk(c, _):
        off = pl.multiple_of(c * ck, ck)
        ikc = ik_ref[0, pl.ds(off, ck), :]
        tiles = [jnp.zeros((tq, LANES), F32)] * n_lt
        for h in range(IDX_HEADS):
            rel = _dot_nt(iq_ref[0, :, h * IDX_DIM:(h + 1) * IDX_DIM], ikc)
            wh = wrep_ref[h]
            tiles = [tiles[t] + jnp.maximum(rel[:, t * LANES:(t + 1) * LANES], 0.0) * wh for t in range(n_lt)]
        score = jnp.concatenate(tiles, axis=1)
        k_pos = off + lax.broadcasted_iota(I32, (tq, ck), 1)
        key_ref[:, pl.ds(off, ck)] = jnp.where(k_pos < limit, _float_key(score), INT_MIN)
        return 0

    lax.fori_loop(0, n_chunks, score_chunk, 0)

    def count_ge(cand):
        def body(c, cnt):
            off = pl.multiple_of(c * ck, ck)
            kc = key_ref[:, pl.ds(off, ck)]
            for t in range(ck // LANES):
                cnt = cnt + jnp.where(kc[:, t * LANES:(t + 1) * LANES] >= cand, 1, 0)
            return cnt
        cnt = lax.fori_loop(0, n_chunks, body, jnp.zeros((tq, LANES), I32))
        return jnp.sum(cnt, axis=1, keepdims=True)

    def bit_step(i, lo):
        cand = lo + lax.shift_left(jnp.int32(1), 31 - i)
        return jnp.where(count_ge(cand) >= top_k, cand, lo)

    lo = lax.fori_loop(0, 32, bit_step, jnp.full((tq, LANES), INT_MIN, I32))
    thr = jnp.maximum(lo, INT_MIN + 1)

    o_ref[0] = jnp.full((tq, s_pad), NEG_INF, o_ref.dtype)

    def write_chunk(c, _):
        off = pl.multiple_of(c * ck, ck)
        kc = key_ref[:, pl.ds(off, ck)]
        o_ref[0, :, pl.ds(off, ck)] = jnp.where(kc >= thr[:, :1], 0.0, NEG_INF).astype(o_ref.dtype)
        return 0

    lax.fori_loop(0, n_chunks, write_chunk, 0)


def _index_call(iq, iw, ik, q_pos0, s_valid, tq, ck):
    bsz, t_q, _ = iq.shape
    s_pad = ik.shape[1]
    top_k = min(IDX_TOPK, s_valid // 4)
    return pl.pallas_call(
        functools.partial(_index_kernel, q_pos0=q_pos0, s_valid=s_valid, tq=tq, ck=ck, top_k=top_k),
        grid=(bsz, t_q // tq),
        in_specs=[
            pl.BlockSpec((1, tq, IDX_WIDTH), lambda b, i: (b, i, 0)),
            pl.BlockSpec((1, tq, LANES), lambda b, i: (b, i, 0)),
            pl.BlockSpec((1, s_pad, IDX_DIM), lambda b, i: (b, 0, 0)),
        ],
        out_specs=pl.BlockSpec((1, tq, s_pad), lambda b, i: (b, i, 0)),
        out_shape=jax.ShapeDtypeStruct((bsz, t_q, s_pad), BF16),
        scratch_shapes=[pltpu.VMEM((tq, s_pad), I32), pltpu.VMEM((IDX_HEADS, tq, LANES), F32)],
        compiler_params=_cparams(("arbitrary", "arbitrary")),
        name="indexer_topk_mask",
    )(iq, iw, ik)


def _diff_kernel(q_ref, k_ref, v_ref, dl_ref, g_ref, o_ref, m_ref, acc_ref, *, q_pos0, s_valid, tq, tk, lam_init):
    iq, ik = pl.program_id(2), pl.program_id(3)
    nk = pl.num_programs(3)
    c = float(C_QK_DIM ** -0.5) * LOG2E

    @pl.when(ik == 0)
    def _():
        m_ref[...] = jnp.full_like(m_ref, NEG_INF)
        acc_ref[...] = jnp.zeros_like(acc_ref)

    last = _last_block(iq, "chunk", q_pos0, tq, tk, s_valid)
    first_masked = (q_pos0 + iq * tq) // tk

    def step(masked):
        if masked:
            q_pos = q_pos0 + iq * tq + lax.broadcasted_iota(I32, (tq, 1), 0)
            k_pos = ik * tk + lax.broadcasted_iota(I32, (tq, tk), 1)
            visible = k_pos < _chunk_limit(q_pos, s_valid)
        for hh in range(HEADS_PER_STEP):
            sl = slice(hh * LANES, (hh + 1) * LANES)
            q = q_ref[0, :, sl]
            k = k_ref[0, :, sl]
            v_ext = _with_ones(v_ref[0, :, sl])
            lane = lax.broadcasted_iota(I32, q.shape, 1)
            for mm, sel in enumerate((lane < C_QK_DIM, lane >= C_QK_DIM)):
                s = _dot_nt(jnp.where(sel, q, jnp.zeros_like(q)), k) * c
                if masked:
                    s = jnp.where(visible, s, NEG_INF)
                _online_softmax_step(s, v_ext, m_ref, acc_ref, 2 * hh + mm)

    @pl.when(ik < jnp.minimum(first_masked, last + 1))
    def _():
        step(False)

    @pl.when((ik >= first_masked) & (ik <= last))
    def _():
        step(True)

    @pl.when(ik == nk - 1)
    def _():
        dl = dl_ref[...]
        lam = (jnp.exp(jnp.sum(dl[0:1] * dl[1:2], axis=1, keepdims=True))
               - jnp.exp(jnp.sum(dl[2:3] * dl[3:4], axis=1, keepdims=True)) + lam_init)
        outs = []
        for hh in range(HEADS_PER_STEP):
            o = _softmax_result(acc_ref, 2 * hh) - lam * _softmax_result(acc_ref, 2 * hh + 1)
            o = o * lax.rsqrt(jnp.mean(o * o, axis=1, keepdims=True) + SUBLN_EPS)
            outs.append(o * g_ref[...] * (1.0 - lam_init))
        o_ref[0] = jnp.concatenate(outs, axis=1).astype(o_ref.dtype)


def _diff_call(q, k, v, dl, g, q_pos0, s_valid, tq, tk, lam_init):
    bsz, t_q, _ = q.shape
    s_pad = k.shape[1]
    hps = HEADS_PER_STEP
    assert C_HEADS % hps == 0
    last = functools.partial(_last_block, mode="chunk", q_pos0=q_pos0, tq=tq, tk=tk, s_valid=s_valid)

    def kv_map(b, hp, iq, ik):
        return (b, jnp.minimum(ik, last(iq)), hp)

    return pl.pallas_call(
        functools.partial(_diff_kernel, q_pos0=q_pos0, s_valid=s_valid, tq=tq, tk=tk, lam_init=lam_init),
        grid=(bsz, C_HEADS // hps, t_q // tq, s_pad // tk),
        in_specs=[
            pl.BlockSpec((1, tq, hps * LANES), lambda b, hp, iq, ik: (b, iq, hp)),
            pl.BlockSpec((1, tk, hps * LANES), kv_map),
            pl.BlockSpec((1, tk, hps * LANES), kv_map),
            pl.BlockSpec((4, C_QK_DIM), lambda b, hp, iq, ik: (0, 0)),
            pl.BlockSpec((1, LANES), lambda b, hp, iq, ik: (0, 0)),
        ],
        out_specs=pl.BlockSpec((1, tq, hps * LANES), lambda b, hp, iq, ik: (b, iq, hp)),
        out_shape=jax.ShapeDtypeStruct((bsz, t_q, C_WIDTH), BF16),
        scratch_shapes=[pltpu.VMEM((2 * hps, tq, LANES), F32), pltpu.VMEM((2 * hps, tq, 2 * LANES), F32)],
        compiler_params=_cparams(("arbitrary",) * 4),
        name="diff_attention",
    )(q, k, v, dl, g)


def _out_kernel(x_ref, oa_ref, ob_ref, oc_ref, w_ref, gt_ref, sc_ref, sh_ref, g_ref, wr_ref, br_ref,
                x1_ref, h_ref, ti_ref, tw_ref):
    attn = (_dot(oa_ref[...], w_ref[0:A_WIDTH, :])
            + _dot(ob_ref[...], w_ref[A_WIDTH:A_WIDTH + B_WIDTH, :])
            + _dot(oc_ref[...], w_ref[A_WIDTH + B_WIDTH:, :]))
    x1 = x_ref[...] + gt_ref[0] * attn
    x1_ref[...] = x1
    h = _rms_mod(x1, g_ref[...], sc_ref[0], sh_ref[0])
    h_ref[...] = h
    logits = _dot(h, wr_ref[...], precision=lax.Precision.HIGHEST) + br_ref[...]
    lane = lax.broadcasted_iota(I32, logits.shape, 1)
    vals, idxs = [], []
    for _ in range(TOP_K):
        mx = jnp.max(logits, axis=1, keepdims=True)
        ix = jnp.min(jnp.where(logits == mx, lane, LANES), axis=1, keepdims=True)
        vals.append(mx)
        idxs.append(ix)
        logits = jnp.where(lane == ix, -jnp.inf, logits)
    es = [jnp.exp(v - vals[0]) for v in vals]
    den = es[0] + es[1] + es[2] + es[3]
    ti = jnp.zeros(lane.shape, I32)
    tw = jnp.zeros(lane.shape, F32)
    for kk in range(TOP_K):
        ti = jnp.where(lane == kk, idxs[kk], ti)
        tw = jnp.where(lane == kk, es[kk] / den, tw)
    ti_ref[...] = ti
    tw_ref[...] = tw


def _out_call(x2d, oa, ob, oc, w_out, gt, sc, sh, g, w_router, b_router, tm, tiles_per_mod):
    n, d = x2d.shape
    r = gt.shape[1]
    row = lambda w: pl.BlockSpec((tm, w), lambda i: (i, 0))
    mod_spec = pl.BlockSpec((1, r, d), lambda i: (i // tiles_per_mod, 0, 0))
    full = lambda a: pl.BlockSpec(a.shape, lambda i: (0,) * a.ndim)
    return pl.pallas_call(
        _out_kernel,
        grid=(n // tm,),
        in_specs=[row(d), row(A_WIDTH), row(B_WIDTH), row(C_WIDTH), full(w_out), mod_spec, mod_spec, mod_spec,
                  full(g), full(w_router), full(b_router)],
        out_specs=[row(d), row(d), row(LANES), row(LANES)],
        out_shape=[jax.ShapeDtypeStruct((n, d), F32), jax.ShapeDtypeStruct((n, d), F32),
                   jax.ShapeDtypeStruct((n, LANES), I32), jax.ShapeDtypeStruct((n, LANES), F32)],
        compiler_params=_cparams(("arbitrary",)),
        name="out_proj_router",
    )(x2d, oa, ob, oc, w_out, gt, sc, sh, g, w_router, b_router)


def _expert_kernel(te_ref, nv_ref, nt_ref, gidx_hbm, sidx_hbm, h_hbm, wg_ref, wu_ref, bg_ref, bu_ref, wd_ref, bd_ref,
                   z_hbm, gidx, sidx, xbuf, xb_ref, acc_ref, ybuf, sem_gi, sem_si, sem_g, sem_s, *, n_tiles):
    t, j = pl.program_id(0), pl.program_id(1)
    nf = pl.num_programs(1)
    used = t < nt_ref[0]
    slot = t % 2
    other = 1 - slot

    def gidx_copy(tt, sl):
        return pltpu.make_async_copy(gidx_hbm.at[tt], gidx.at[sl], sem_gi.at[sl])

    def sidx_copy(tt, sl):
        return pltpu.make_async_copy(sidx_hbm.at[tt], sidx.at[sl], sem_si.at[sl])

    def gather_row(sl, r, token):
        return pltpu.make_async_copy(h_hbm.at[pl.ds(token, 1)], xbuf.at[sl, pl.ds(r, 1)], sem_g.at[sl])

    def scatter_row(sl, r, dest):
        return pltpu.make_async_copy(ybuf.at[sl, pl.ds(r, 1)], z_hbm.at[pl.ds(dest, 1)], sem_s.at[sl])

    def for_valid_rows(tt, body):
        def step(r, _):
            body(r)
            return 0
        lax.fori_loop(0, nv_ref[tt], step, 0)

    def start_gather(tt, sl):
        for_valid_rows(tt, lambda r: gather_row(sl, r, gidx[sl, r]).start())

    def wait_gather(tt, sl):
        for_valid_rows(tt, lambda r: gather_row(sl, 0, 0).wait())

    def start_scatter(tt, sl):
        for_valid_rows(tt, lambda r: scatter_row(sl, r, sidx[sl, r]).start(priority=1))

    def wait_scatter(tt, sl):
        for_valid_rows(tt, lambda r: scatter_row(sl, 0, 0).wait())

    @pl.when((t == 0) & (j == 0))
    def _():
        xbuf[...] = jnp.zeros_like(xbuf)
        gidx_copy(0, 0).start()
        gidx_copy(0, 0).wait()
        start_gather(0, 0)
        if n_tiles > 1:
            gidx_copy(1, 1).start()

    @pl.when(j == 0)
    def _():
        wait_gather(t, slot)

        @pl.when(t + 1 < n_tiles)
        def _():
            gidx_copy(t + 1, other).wait()
            start_gather(t + 1, other)

        @pl.when(t + 2 < n_tiles)
        def _():
            gidx_copy(t + 2, slot).start()

        sidx_copy(t, slot).start()

    @pl.when(used & (j == 0))
    def _():
        xb_ref[...] = xbuf[slot].astype(BF16)
        acc_ref[...] = jnp.zeros_like(acc_ref)

    @pl.when(used)
    def _():
        x = xb_ref[...]
        gate = jnp.minimum(_dot(x, wg_ref[0]) + bg_ref[0], SWIGLU_LIMIT)
        up = jnp.clip(_dot(x, wu_ref[0]) + bu_ref[0], -SWIGLU_LIMIT, SWIGLU_LIMIT)
        act = (up + 1.0) * (gate * jax.nn.sigmoid(gate * SWIGLU_ALPHA))
        acc_ref[...] += _dot(act.astype(BF16), wd_ref[0])

    @pl.when(j == nf - 1)
    def _():
        @pl.when(t >= 2)
        def _():
            wait_scatter(t - 2, slot)

        @pl.when(used)
        def _():
            ybuf[slot] = acc_ref[...] + bd_ref[0]

        sidx_copy(t, slot).wait()
        start_scatter(t, slot)

        @pl.when(t == n_tiles - 1)
        def _():
            @pl.when(t >= 1)
            def _():
                wait_scatter(t - 1, other)

            wait_scatter(t, slot)


def _expert_call(h_all, src_row, dest_row, tile_expert, n_valid, n_tiles_used, w_gu, b_gu, w_d, b_d, tm, tf):
    n, d = h_all.shape
    n_exp, _, f2 = w_gu.shape
    n_tiles = src_row.shape[0] // tm
    f = f2 // 2
    nf = f // tf
    any_spec = pl.BlockSpec(memory_space=pl.ANY)
    grid_spec = pltpu.PrefetchScalarGridSpec(
        num_scalar_prefetch=3,
        grid=(n_tiles, nf),
        in_specs=[
            any_spec, any_spec, any_spec,
            pl.BlockSpec((1, d, tf), lambda t, j, te, nv, nt: (te[t], 0, j)),
            pl.BlockSpec((1, d, tf), lambda t, j, te, nv, nt: (te[t], 0, nf + j)),
            pl.BlockSpec((1, 1, tf), lambda t, j, te, nv, nt: (te[t], 0, j)),
            pl.BlockSpec((1, 1, tf), lambda t, j, te, nv, nt: (te[t], 0, nf + j)),
            pl.BlockSpec((1, tf, d), lambda t, j, te, nv, nt: (te[t], j, 0)),
            pl.BlockSpec((1, 1, d), lambda t, j, te, nv, nt: (te[t], 0, 0)),
        ],
        out_specs=any_spec,
        scratch_shapes=[
            pltpu.SMEM((2, tm), I32), pltpu.SMEM((2, tm), I32),
            pltpu.VMEM((2, tm, d), F32), pltpu.VMEM((tm, d), BF16), pltpu.VMEM((tm, d), F32),
            pltpu.VMEM((2, tm, d), F32),
            pltpu.SemaphoreType.DMA((2,)), pltpu.SemaphoreType.DMA((2,)),
            pltpu.SemaphoreType.DMA((2,)), pltpu.SemaphoreType.DMA((2,)),
        ],
    )
    return pl.pallas_call(
        functools.partial(_expert_kernel, n_tiles=n_tiles),
        grid_spec=grid_spec,
        out_shape=jax.ShapeDtypeStruct((TOP_K * n, d), F32),
        compiler_params=_cparams(("arbitrary", "arbitrary")),
        name="moe_experts",
    )(tile_expert, n_valid, n_tiles_used, src_row.reshape(n_tiles, tm), dest_row.reshape(n_tiles, tm), h_all,
      w_gu, w_gu, b_gu.reshape(n_exp, 1, f2), b_gu.reshape(n_exp, 1, f2), w_d, b_d.reshape(n_exp, 1, d))


def _combine_kernel(x_ref, z0, z1, z2, z3, tw_ref, gt_ref, gf_ref, o_ref, *, final_norm):
    tw = tw_ref[...]
    moe = jnp.zeros(x_ref.shape, F32)
    for kk, z in enumerate((z0, z1, z2, z3)):
        moe = moe + tw[:, kk:kk + 1] * z[0]
    x2 = x_ref[...] + gt_ref[0] * moe
    if final_norm:
        x2 = x2 * lax.rsqrt(jnp.mean(x2 * x2, axis=-1, keepdims=True) + RMS_EPS) * gf_ref[...]
    o_ref[...] = x2


def _combine_call(z, tok0, x1, top_w, gt, tiles_per_mod, g_final, tn, final_norm):
    n, d = x1.shape
    r = gt.shape[1]
    blk0 = tok0 // tn
    z_specs = [pl.BlockSpec((1, tn, d), lambda i, kk=kk: (kk, blk0 + i, 0)) for kk in range(TOP_K)]
    return pl.pallas_call(
        functools.partial(_combine_kernel, final_norm=final_norm),
        grid=(n // tn,),
        in_specs=[pl.BlockSpec((tn, d), lambda i: (i, 0))] + z_specs + [
            pl.BlockSpec((tn, LANES), lambda i: (i, 0)),
            pl.BlockSpec((1, r, d), lambda i: (i // tiles_per_mod, 0, 0)),
            pl.BlockSpec((1, d), lambda i: (0, 0)),
        ],
        out_specs=pl.BlockSpec((tn, d), lambda i: (i, 0)),
        out_shape=jax.ShapeDtypeStruct((n, d), F32),
        compiler_params=_cparams(("arbitrary",)),
        name="moe_combine",
    )(x1, z, z, z, z, top_w, gt, g_final)


def _routing_plan(top_i, n_exp, tm):
    n = top_i.shape[0]
    m = n * TOP_K
    flat = top_i.reshape(-1)
    order = jnp.argsort(flat, stable=True).astype(I32)
    sorted_e = flat[order]
    experts = jnp.arange(n_exp, dtype=I32)
    end = jnp.sum(sorted_e[None, :] <= experts[:, None], axis=1).astype(I32)
    counts = jnp.diff(end, prepend=0)
    start = end - counts
    padded = (counts + tm - 1) // tm * tm
    pend = jnp.cumsum(padded)
    pstart = pend - padded
    p = (m + n_exp * (tm - 1)) // tm * tm
    rows = jnp.arange(p, dtype=I32)
    e_row = jnp.minimum(jnp.sum(rows[:, None] >= pend[None, :], axis=1), n_exp - 1).astype(I32)
    rank = rows - pstart[e_row]
    valid = rank < counts[e_row]
    pair = order[jnp.clip(start[e_row] + rank, 0, m - 1)]
    src_row = jnp.where(valid, pair // TOP_K, 0)
    dest_row = jnp.where(valid, (pair % TOP_K) * n + pair // TOP_K, 0)
    tile_expert = e_row[::tm]
    n_valid = jnp.clip(counts[tile_expert] - rank[::tm], 0, tm).astype(I32)
    n_tiles_used = (pend[-1] // tm).astype(I32).reshape(1)
    return src_row, dest_row, tile_expert, n_valid, n_tiles_used


def _pad_rows(a, axis, size):
    pad = size - a.shape[axis]
    if pad == 0:
        return a
    widths = [(0, 0)] * a.ndim
    widths[axis] = (0, pad)
    return jnp.pad(a, widths)


def _mixers(pr, new_logf, past, q_pos0, dl, g_diff, lam_init, tq, tk, tq_idx, ck_idx):
    bsz, t = pr["fq_b"].shape[:2]
    keys = {}
    if past is None:
        for nm in ("fk", "fv", "dk", "dv", "ik", "ck", "cv"):
            keys[nm] = pr[nm + "_b"]
        logf_all = new_logf
        s_valid = t
    else:
        s_valid = past["fk"].shape[1] + t
        s_pad = -(-s_valid // ck_idx) * ck_idx
        tk = s_pad
        for nm in ("fk", "fv", "dk", "dv", "ik", "ck", "cv"):
            cache = past[nm].reshape(bsz, past[nm].shape[1], -1).astype(BF16)
            keys[nm] = _pad_rows(jnp.concatenate([cache, pr[nm + "_b"]], axis=1), 1, s_pad)
        logf_all = jnp.concatenate([past["logf"], new_logf], axis=1)
    s_pad = keys["fk"].shape[1]

    chan = jnp.transpose(logf_all, (1, 0, 2)).reshape(logf_all.shape[1], bsz * A_HEADS)
    cum = _cumsum_call(_pad_rows(_pad_rows(chan, 1, LANES), 0, s_pad))
    o_a = _flash_call("fox", pr["fq_b"], keys["fk"], keys["fv"], (cum, cum.T), A_HEADS, q_pos0, s_valid, tq, tk)

    bias = _index_call(pr["iq_b"], pr["small"], keys["ik"], q_pos0, s_valid, tq_idx, ck_idx)
    o_b = _flash_call("dsa", pr["dq_b"], keys["dk"], keys["dv"], (bias,), B_HEADS, q_pos0, s_valid, tq, tk)

    o_c = _diff_call(pr["cq_b"], keys["ck"], keys["cv"], dl, g_diff, q_pos0, s_valid, tq, tk, lam_init)
    return o_a, o_b, o_c


def kernel(x_prompt, x_sample, cache_fox_k, cache_fox_v, cache_fox_logf, cache_dsa_k, cache_dsa_v,
           cache_dsa_idx_k, cache_diff_k, cache_diff_v, c_prompt, c_sample, w_ada, b_ada, g_attn, g_ffn,
           w_in, b_forget, diff_lambda, g_diff, w_out, w_router, b_router, w_gate_up, b_gate_up,
           w_down, b_down, g_final):
    depth = w_ada.shape[0]
    bp, tp, d = x_prompt.shape
    bs, ts, _ = x_sample.shape
    past_len = cache_fox_k.shape[2]
    n_exp = w_router.shape[-1]
    np_, ns = bp * tp, bs * ts

    tm_p = min(512, tp)
    tm_o = min(512, tp)
    tq = min(512, tp)
    tk = min(1024, tp)
    tq_idx = min(128, tp)
    ck_idx = min(512, tp)
    tm_e = 512
    tn_c = min(256, ns)

    m_rows = -(-(bp + bs) // 16) * 16
    c_all = _pad_rows(jnp.concatenate([c_prompt, c_sample], axis=0), 0, m_rows)
    mod = _ada_call(c_all, w_ada, b_ada)

    tabs_p = _rope_tables(jnp.arange(tp, dtype=I32))
    tabs_s = [jnp.tile(tb, (bs, 1)) for tb in _rope_tables(jnp.arange(past_len, past_len + ts, dtype=I32))]

    offs = {}
    acc = 0
    for nm, wdt in (("fq", A_WIDTH), ("fk", A_WIDTH), ("fv", A_WIDTH), ("ff", A_HEADS), ("dq", B_WIDTH),
                    ("dk", B_WIDTH), ("dv", B_WIDTH), ("iq", IDX_WIDTH), ("ik", IDX_DIM), ("iw", IDX_HEADS),
                    ("cq", C_WIDTH), ("ck", C_WIDTH), ("cv", C_WIDTH)):
        offs[nm] = (acc, acc + wdt)
        acc += wdt
    col_order = ("fq", "fk", "fv", "dq", "dk", "dv", "iq", "cq", "ck", "cv", "ik", "ff", "iw")

    xp = x_prompt.reshape(np_, d)
    xs = x_sample.reshape(ns, d)
    rows_p, rows_s = [], []
    for l in range(depth):
        lam_init = 0.8 - 0.6 * math.exp(-0.3 * l)
        w_cat = jnp.concatenate([w_in[l][:, offs[nm][0]:offs[nm][1]] for nm in col_order], axis=1)
        w_cat = _pad_rows(w_cat, 1, _PROJ_TILES * PROJ_TN).astype(BF16)
        b_small = _pad_rows(b_forget[l].reshape(1, A_HEADS), 1, LANES)
        w_out_b = w_out[l].astype(BF16)
        w_r = _pad_rows(w_router[l], 1, LANES)
        b_r = jnp.concatenate([b_router[l], jnp.full((LANES - n_exp,), NEG_INF, F32)]).reshape(1, LANES)
        w_gu_b = w_gate_up[l].astype(BF16)
        w_d_b = w_down[l].astype(BF16)

        mods_p = [m.reshape(bp, 1, d) for m in jnp.split(mod[l, :bp], 6, axis=-1)]
        mods_s = [jnp.broadcast_to(m[:, None, :], (bs, ts, d)).reshape(ns, d)
                  for m in jnp.split(mod[l, bp:bp + bs], 6, axis=-1)]

        def mod_view(m, t, tile):
            if m.ndim == 3:
                return m, t // tile
            return m.reshape(m.shape[0] // tile, tile, d), 1

        groups = []
        for (x2d, mods, tabs, bsz, t, tm, past, q_pos0, tqg, tqi) in (
                (xp, mods_p, tabs_p, bp, tp, tm_p, None, 0, tq, tq_idx),
                (xs, mods_s, tabs_s, bs, ts, ns,
                 dict(fk=cache_fox_k[l], fv=cache_fox_v[l], logf=cache_fox_logf[l], dk=cache_dsa_k[l],
                      dv=cache_dsa_v[l], ik=cache_dsa_idx_k[l], ck=cache_diff_k[l], cv=cache_diff_v[l]),
                 past_len, ts, ts)):
            sh1, sc1, gt1, sh2, sc2, gt2 = mods
            (sc1v, tpm), (sh1v, _) = mod_view(sc1, t, tm), mod_view(sh1, t, tm)
            pr = _proj_call(x2d, sc1v, sh1v, g_attn[l].reshape(1, d), w_cat, tabs, b_small, tm, tpm)
            pr3 = {nm: a.reshape(bsz, t, a.shape[-1]) for nm, a in pr.items()}
            new_logf = pr3["small"][:, :, :A_HEADS]
            o_a, o_b, o_c = _mixers(pr3, new_logf, past, q_pos0, diff_lambda[l], g_diff[l].reshape(1, LANES),
                                    lam_init, tqg, tk, tqi, ck_idx)
            tmo = min(tm_o, bsz * t)
            (gt1v, tpm_o), (sc2v, _), (sh2v, _) = (mod_view(m, t, tmo) for m in (gt1, sc2, sh2))
            x1, h2, top_i, top_w = _out_call(
                x2d, o_a.reshape(bsz * t, -1), o_b.reshape(bsz * t, -1), o_c.reshape(bsz * t, -1), w_out_b,
                gt1v, sc2v, sh2v, g_ffn[l].reshape(1, d), w_r, b_r, tmo, tpm_o)
            rows = (pr3["fk_f"].reshape(bsz, t, A_HEADS, HEAD_DIM), pr3["fv_f"].reshape(bsz, t, A_HEADS, HEAD_DIM),
                    new_logf, pr3["dk_f"].reshape(bsz, t, B_HEADS, HEAD_DIM),
                    pr3["dv_f"].reshape(bsz, t, B_HEADS, HEAD_DIM), pr3["ik_f"],
                    pr3["ck_f"].reshape(bsz, t, C_HEADS, 2, C_QK_DIM), pr3["cv_f"].reshape(bsz, t, C_HEADS, 2 * C_QK_DIM))
            groups.append((x1, h2, top_i, top_w, gt2, rows))

        (x1p, h2p, tip, twp, gt2p, rp), (x1s, h2s, tis, tws, gt2s, rs) = groups
        rows_p.append(rp)
        rows_s.append(rs)

        h_all = jnp.concatenate([h2p, h2s], axis=0)
        ti_all = jnp.concatenate([tip[:, :TOP_K], tis[:, :TOP_K]], axis=0)
        src_row, dest_row, tile_expert, n_valid, n_tiles_used = _routing_plan(ti_all, n_exp, tm_e)
        z = _expert_call(h_all, src_row, dest_row, tile_expert, n_valid, n_tiles_used, w_gu_b, b_gate_up[l],
                         w_d_b, b_down[l], tm_e, min(512, w_d_b.shape[1]))
        z = z.reshape(TOP_K, np_ + ns, d)
        final = l == depth - 1
        gf = g_final.reshape(1, d)
        xp = _combine_call(z, 0, x1p, twp, *mod_view(gt2p, tp, tn_c), gf, tn_c, final)
        xs = _combine_call(z, np_, x1s, tws, *mod_view(gt2s, ts, tn_c), gf, tn_c, final)

    def stack(rows, i):
        return jnp.stack([r[i] for r in rows], axis=0)

    return ((xp.reshape(bp, tp, d), xs.reshape(bs, ts, d))
            + tuple(stack(rows_p, i) for i in range(8)) + tuple(stack(rows_s, i) for i in range(8)))
```

```python
import functools
import math

import jax
import jax.numpy as jnp
from jax import lax
from jax.experimental import pallas as pl
from jax.experimental.pallas import tpu as pltpu

F32 = jnp.float32
BF16 = jnp.bfloat16
I32 = jnp.int32

LANES = 128
HEAD_DIM = 128
A_HEADS = 6
B_HEADS = 6
C_HEADS = 4
C_QK_DIM = 64
IDX_HEADS = 16
IDX_DIM = 128
IDX_TOPK = 256
CHUNK = 64
ROPE_THETA = 500000.0
ROPE_FRACTION = 4
TOP_K = 4
SWIGLU_LIMIT = 7.0
SWIGLU_ALPHA = 1.702
RMS_EPS = 1e-6
SUBLN_EPS = 1e-5
NEG_INF = -1e30
INT_MIN = -2 ** 31
LOG2E = 1.4426950408889634
HEADS_PER_STEP = 2
ROW_UNROLL = 8

A_WIDTH = A_HEADS * HEAD_DIM
B_WIDTH = B_HEADS * HEAD_DIM
C_WIDTH = C_HEADS * 2 * C_QK_DIM
IDX_WIDTH = IDX_HEADS * IDX_DIM
N_SMALL_FORGET = A_HEADS
N_SMALL_IW = IDX_HEADS

PROJ_TN = 256
VMEM_LIMIT = 56 * 1024 * 1024


def _cparams(sem):
    return pltpu.CompilerParams(dimension_semantics=sem, vmem_limit_bytes=VMEM_LIMIT)


def _dot_nt(a, b):
    return lax.dot_general(a, b, (((1,), (1,)), ((), ())), preferred_element_type=F32)


def _dot(a, b, precision=None):
    return jnp.dot(a, b, preferred_element_type=F32, precision=precision)


def _ada_kernel(c_ref, w_ref, b_ref, o_ref):
    c = c_ref[...]
    a = (c * jax.nn.sigmoid(c)).astype(BF16)
    o_ref[0] = _dot(a, w_ref[0].astype(BF16)) + b_ref[0]


def _ada_call(c_all, w_ada, b_ada):
    depth, d, n = w_ada.shape
    m = c_all.shape[0]
    tn = 512
    return pl.pallas_call(
        _ada_kernel,
        grid=(depth, n // tn),
        in_specs=[
            pl.BlockSpec((m, d), lambda l, j: (0, 0)),
            pl.BlockSpec((1, d, tn), lambda l, j: (l, 0, j)),
            pl.BlockSpec((1, 1, tn), lambda l, j: (l, 0, j)),
        ],
        out_specs=pl.BlockSpec((1, m, tn), lambda l, j: (l, 0, j)),
        out_shape=jax.ShapeDtypeStruct((depth, m, n), F32),
        compiler_params=_cparams(("arbitrary", "arbitrary")),
        name="ada_mod",
    )(c_all, w_ada, b_ada.reshape(depth, 1, n))


def _cumsum_kernel(x_ref, o_ref, carry_ref):
    @pl.when(pl.program_id(0) == 0)
    def _():
        carry_ref[...] = jnp.zeros_like(carry_ref)

    tb = x_ref.shape[0]
    row = lax.broadcasted_iota(I32, (tb, tb), 0)
    col = lax.broadcasted_iota(I32, (tb, tb), 1)
    tri = jnp.where(col <= row, 1.0, 0.0).astype(F32)
    cs = _dot(tri, x_ref[...], precision=lax.Precision.HIGHEST) + carry_ref[0:1, :]
    o_ref[...] = cs
    carry_ref[...] = jnp.broadcast_to(cs[tb - 1:tb, :], carry_ref.shape)


def _cumsum_call(x):
    s = x.shape[0]
    tb = 256
    return pl.pallas_call(
        _cumsum_kernel,
        grid=(s // tb,),
        in_specs=[pl.BlockSpec((tb, LANES), lambda i: (i, 0))],
        out_specs=pl.BlockSpec((tb, LANES), lambda i: (i, 0)),
        out_shape=jax.ShapeDtypeStruct((s, LANES), F32),
        scratch_shapes=[pltpu.VMEM((8, LANES), F32)],
        compiler_params=_cparams(("arbitrary",)),
        name="forget_cumsum",
    )(x)


_PROJ_PIECES = (
    ("fq", 0, 3, None, False, True),
    ("fk", 3, 3, None, True, True),
    ("fv", 6, 3, None, True, True),
    ("dq", 9, 3, "r128", False, True),
    ("dk", 12, 3, "r128", True, True),
    ("dv", 15, 3, None, True, True),
    ("iq", 18, 8, "r128", False, True),
    ("cq", 26, 2, "r64", False, True),
    ("ck", 28, 2, "r64", True, True),
    ("cv", 30, 2, None, True, True),
)
_PROJ_LAST_TILE = 32
_PROJ_TILES = 33


def _rms_mod(x, g, sc, sh):
    y = x * lax.rsqrt(jnp.mean(x * x, axis=-1, keepdims=True) + RMS_EPS)
    return (y * g) * (1.0 + sc) + sh


def _rope_tile(z, c, s1, s2, shift):
    return z * c + pltpu.roll(z, LANES - shift, 1) * s1 + pltpu.roll(z, shift, 1) * s2


def _proj_kernel(x_ref, sc_ref, sh_ref, g_ref, w_ref, c128, s1128, s2128, c64, s164, s264, bs_ref,
                 *rest, iw_scale):
    outs = rest[:-1]
    h_s = rest[-1]
    j = pl.program_id(1)

    @pl.when(j == 0)
    def _():
        h_s[...] = _rms_mod(x_ref[...], g_ref[...], sc_ref[0], sh_ref[0]).astype(BF16)

    z = _dot(h_s[...], w_ref[...])

    def rope(zz, kind):
        if kind is None:
            return zz
        tabs, shift = ((c128, s1128, s2128), 16) if kind == "r128" else ((c64, s164, s264), 8)
        c, s1, s2 = (t[...] for t in tabs)
        return jnp.concatenate(
            [_rope_tile(zz[:, t * LANES:(t + 1) * LANES], c, s1, s2, shift) for t in range(2)], axis=1)

    k = 0
    for (_, start, cnt, kind, wf, wb) in _PROJ_PIECES:
        refs = outs[k:k + int(wf) + int(wb)]
        k += int(wf) + int(wb)

        @pl.when((j >= start) & (j < start + cnt))
        def _(refs=refs, kind=kind):
            r = rope(z, kind)
            for o in refs:
                o[...] = r.astype(o.dtype)

    ik_f, ik_b, small = outs[k:k + 3]

    @pl.when(j == _PROJ_LAST_TILE)
    def _():
        r = _rope_tile(z[:, :LANES], c128[...], s1128[...], s2128[...], 16)
        ik_f[...] = r
        ik_b[...] = r.astype(BF16)
        zs = z[:, LANES:]
        lane = lax.broadcasted_iota(I32, zs.shape, 1)
        zf = zs + bs_ref[...]
        logsig = jnp.minimum(zf, 0.0) - jnp.log(1.0 + jnp.exp(-jnp.abs(zf)))
        small[...] = jnp.where(lane < N_SMALL_FORGET, logsig,
                               jnp.where(lane < N_SMALL_FORGET + N_SMALL_IW, zs * iw_scale, zs))


def _proj_call(x2d, sc, sh, g, w_cat, tabs, b_small, tm, tiles_per_mod):
    n, d = x2d.shape
    r = sc.shape[1]
    ntab = tabs[0].shape[0] // tm
    out_shapes, out_specs = [], []

    def piece_spec(start, cnt, width):
        return pl.BlockSpec((tm, width), lambda i, j: (i, jnp.clip(j - start, 0, cnt - 1)))

    for (_, start, cnt, _, wf, wb) in _PROJ_PIECES:
        for dt, on in ((F32, wf), (BF16, wb)):
            if on:
                out_shapes.append(jax.ShapeDtypeStruct((n, cnt * PROJ_TN), dt))
                out_specs.append(piece_spec(start, cnt, PROJ_TN))
    for dt in (F32, BF16, F32):
        out_shapes.append(jax.ShapeDtypeStruct((n, LANES), dt))
        out_specs.append(pl.BlockSpec((tm, LANES), lambda i, j: (i, 0)))

    mod_spec = pl.BlockSpec((1, r, d), lambda i, j: (i // tiles_per_mod, 0, 0))
    tab_spec = pl.BlockSpec((tm, LANES), lambda i, j: (i % ntab, 0))
    outs = pl.pallas_call(
        functools.partial(_proj_kernel, iw_scale=float(IDX_HEADS ** -0.5 * IDX_DIM ** -0.5)),
        grid=(n // tm, _PROJ_TILES),
        in_specs=[
            pl.BlockSpec((tm, d), lambda i, j: (i, 0)),
            mod_spec, mod_spec,
            pl.BlockSpec((1, d), lambda i, j: (0, 0)),
            pl.BlockSpec((d, PROJ_TN), lambda i, j: (0, j)),
            tab_spec, tab_spec, tab_spec, tab_spec, tab_spec, tab_spec,
            pl.BlockSpec((1, LANES), lambda i, j: (0, 0)),
        ],
        out_specs=out_specs,
        out_shape=out_shapes,
        scratch_shapes=[pltpu.VMEM((tm, d), BF16)],
        compiler_params=_cparams(("arbitrary", "arbitrary")),
        name="norm_in_proj",
    )(x2d, sc, sh, g, w_cat, *tabs, b_small)
    names = []
    for (nm, _, _, _, wf, wb) in _PROJ_PIECES:
        if wf:
            names.append(nm + "_f")
        if wb:
            names.append(nm + "_b")
    names += ["ik_f", "ik_b", "small"]
    return dict(zip(names, outs))


def _rope_tables(pos):
    posf = pos.astype(F32)[:, None]
    lane = jnp.arange(LANES)
    tabs = []
    for hd in (HEAD_DIM, C_QK_DIM):
        half = hd // ROPE_FRACTION // 2
        inv_freq = ROPE_THETA ** (-jnp.arange(half, dtype=F32) / half)
        ang = posf * inv_freq[None, :]
        cos, sin = jnp.cos(ang), jnp.sin(ang)
        off = lane % hd
        first = off < half
        second = (off >= half) & (off < 2 * half)
        idx = jnp.where(first, off, jnp.where(second, off - half, 0))
        cos_l, sin_l = cos[:, idx], sin[:, idx]
        tabs.append(jnp.where((first | second)[None, :], cos_l, 1.0))
        tabs.append(jnp.where(first[None, :], -sin_l, 0.0))
        tabs.append(jnp.where(second[None, :], sin_l, 0.0))
    return tabs


def _chunk_limit(q_pos, s_valid):
    return jnp.minimum((q_pos // CHUNK + 1) * CHUNK, s_valid)


def _last_block(iq, mode, q_pos0, tq, tk, s_valid):
    q_last = q_pos0 + (iq + 1) * tq - 1
    if mode == "fox":
        k_last = jnp.minimum(q_last, s_valid - 1)
    else:
        k_last = _chunk_limit(q_last, s_valid) - 1
    return k_last // tk


def _lane_tiles(x):
    return [x[:, t * LANES:(t + 1) * LANES] for t in range(x.shape[1] // LANES)]


def _add_rep(s, rep):
    return jnp.concatenate([st + rep for st in _lane_tiles(s)], axis=1)


def _with_ones(v):
    return jnp.concatenate([v, jnp.ones_like(v)], axis=1)


def _online_softmax_step(s, v_ext, m_ref, acc_ref, i):
    m_prev = m_ref[i]
    m_next = jnp.maximum(m_prev, jnp.max(s, axis=1, keepdims=True))
    alpha = jnp.exp2(m_prev - m_next)
    p = jnp.concatenate([jnp.exp2(st - m_next) for st in _lane_tiles(s)], axis=1)
    acc_ref[i] = acc_ref[i] * jnp.concatenate([alpha, alpha], axis=1) + _dot(p.astype(v_ext.dtype), v_ext)
    m_ref[i] = m_next


def _softmax_result(acc_ref, i):
    acc = acc_ref[i]
    return acc[:, :HEAD_DIM] / acc[:, HEAD_DIM:]


def _flash_kernel(*refs, mode, q_pos0, s_valid, tq, tk, n_heads):
    if mode == "fox":
        q_ref, k_ref, v_ref, cq_ref, ck_ref, o_ref, m_ref, acc_ref, fq_ref = refs
    else:
        q_ref, k_ref, v_ref, bias_ref, o_ref, m_ref, acc_ref = refs
    b, hp, iq, ik = (pl.program_id(a) for a in range(4))
    nk = pl.num_programs(3)
    ch0 = b * n_heads + hp * HEADS_PER_STEP
    c = float(HEAD_DIM ** -0.5) * LOG2E

    @pl.when(ik == 0)
    def _():
        m_ref[...] = jnp.full_like(m_ref, NEG_INF)
        acc_ref[...] = jnp.zeros_like(acc_ref)
        if mode == "fox":
            lane = lax.broadcasted_iota(I32, cq_ref.shape, 1)
            cq = cq_ref[...]
            for hh in range(HEADS_PER_STEP):
                col = jnp.sum(jnp.where(lane == ch0 + hh, cq, 0.0), axis=1, keepdims=True)
                fq_ref[hh] = jnp.broadcast_to(col * LOG2E, (tq, LANES))

    last = _last_block(iq, mode, q_pos0, tq, tk, s_valid)

    def step(masked):
        if masked:
            q_pos = q_pos0 + iq * tq + lax.broadcasted_iota(I32, (tq, tk), 0)
            k_pos = ik * tk + lax.broadcasted_iota(I32, (tq, tk), 1)
            visible = k_pos <= q_pos
        for hh in range(HEADS_PER_STEP):
            sl = slice(hh * HEAD_DIM, (hh + 1) * HEAD_DIM)
            s = _dot_nt(q_ref[0, :, sl], k_ref[0, :, sl]) * c
            if mode == "fox":
                fk = ck_ref[pl.ds(ch0 % 8 + hh, 1), :] * LOG2E
                s = _add_rep(s, fq_ref[hh]) - fk
                if masked:
                    s = jnp.where(visible, s, NEG_INF)
            else:
                s = s + bias_ref[0].astype(F32)
            _online_softmax_step(s, _with_ones(v_ref[0, :, sl]), m_ref, acc_ref, hh)

    if mode == "fox":
        first_masked = (q_pos0 + iq * tq) // tk

        @pl.when(ik < jnp.minimum(first_masked, last + 1))
        def _():
            step(False)

        @pl.when((ik >= first_masked) & (ik <= last))
        def _():
            step(True)
    else:
        @pl.when(ik <= last)
        def _():
            step(False)

    @pl.when(ik == nk - 1)
    def _():
        o_ref[0] = jnp.concatenate([_softmax_result(acc_ref, hh) for hh in range(HEADS_PER_STEP)],
                                   axis=1).astype(o_ref.dtype)


def _flash_call(mode, q, k, v, extra, n_heads, q_pos0, s_valid, tq, tk):
    bsz, t_q, _ = q.shape
    s_pad = k.shape[1]
    nq, nk = t_q // tq, s_pad // tk
    hps = HEADS_PER_STEP
    assert n_heads % hps == 0 and 8 % hps == 0
    last = functools.partial(_last_block, mode=mode, q_pos0=q_pos0, tq=tq, tk=tk, s_valid=s_valid)

    def kv_map(b, hp, iq, ik):
        return (b, jnp.minimum(ik, last(iq)), hp)

    in_specs = [
        pl.BlockSpec((1, tq, hps * HEAD_DIM), lambda b, hp, iq, ik: (b, iq, hp)),
        pl.BlockSpec((1, tk, hps * HEAD_DIM), kv_map),
        pl.BlockSpec((1, tk, hps * HEAD_DIM), kv_map),
    ]
    scratch = [pltpu.VMEM((hps, tq, LANES), F32), pltpu.VMEM((hps, tq, 2 * HEAD_DIM), F32)]
    if mode == "fox":
        cum, cum_t = extra
        qb0 = q_pos0 // tq
        in_specs += [
            pl.BlockSpec((tq, LANES), lambda b, hp, iq, ik: (qb0 + iq, 0)),
            pl.BlockSpec((8, tk), lambda b, hp, iq, ik: ((b * n_heads + hp * hps) // 8, jnp.minimum(ik, last(iq)))),
        ]
        scratch.append(pltpu.VMEM((hps, tq, LANES), F32))
        args = (q, k, v, cum, cum_t)
    else:
        (bias,) = extra
        in_specs.append(pl.BlockSpec((1, tq, tk), lambda b, hp, iq, ik: (b, iq, jnp.minimum(ik, last(iq)))))
        args = (q, k, v, bias)
    return pl.pallas_call(
        functools.partial(_flash_kernel, mode=mode, q_pos0=q_pos0, s_valid=s_valid, tq=tq, tk=tk, n_heads=n_heads),
        grid=(bsz, n_heads // hps, nq, nk),
        in_specs=in_specs,
        out_specs=pl.BlockSpec((1, tq, hps * HEAD_DIM), lambda b, hp, iq, ik: (b, iq, hp)),
        out_shape=jax.ShapeDtypeStruct((bsz, t_q, n_heads * HEAD_DIM), BF16),
        scratch_shapes=scratch,
        compiler_params=_cparams(("arbitrary",) * 4),
        name=mode + "_attention",
    )(*args)


def _float_key(x):
    bits = pltpu.bitcast(x + 0.0, I32)
    return jnp.where(bits < 0, bits ^ jnp.int32(0x7FFFFFFF), bits)


def _index_kernel(iq_ref, iw_ref, ik_ref, o_ref, key_ref, wrep_ref, *, q_pos0, s_valid, tq, ck, top_k):
    iq = pl.program_id(1)
    s_pad = ik_ref.shape[1]
    q_pos = q_pos0 + iq * tq + lax.broadcasted_iota(I32, (tq, 1), 0)
    limit = _chunk_limit(q_pos, s_valid)
    n_adm = _chunk_limit(q_pos0 + (iq + 1) * tq - 1, s_valid)
    n_chunks = (n_adm + ck - 1) // ck

    iw = iw_ref[0]
    for h in range(IDX_HEADS):
        wrep_ref[h] = jnp.broadcast_to(iw[:, N_SMALL_FORGET + h:N_SMALL_FORGET + h + 1], (tq, LANES))
    n_lt = ck // LANES

    def score_chunk(c, _):
        off = pl.multiple_of(c * ck, ck)
        ikc = ik_ref[0, pl.ds(off, ck), :]
        tiles = [jnp.zeros((tq, LANES), F32)] * n_lt
        for h in range(IDX_HEADS):
            rel = _dot_nt(iq_ref[0, :, h * IDX_DIM:(h + 1) * IDX_DIM], ikc)
            wh = wrep_ref[h]
            tiles = [tiles[t] + jnp.maximum(rel[:, t * LANES:(t + 1) * LANES], 0.0) * wh for t in range(n_lt)]
        score = jnp.concatenate(tiles, axis=1)
        k_pos = off + lax.broadcasted_iota(I32, (tq, ck), 1)
        key_ref[:, pl.ds(off, ck)] = jnp.where(k_pos < limit, _float_key(score), INT_MIN)
        return 0

    lax.fori_loop(0, n_chunks, score_chunk, 0)

    def count_ge(cand):
        def body(c, cnt):
            off = pl.multiple_of(c * ck, ck)
            kc = key_ref[:, pl.ds(off, ck)]
            for t in range(ck // LANES):
                cnt = cnt + jnp.where(kc[:, t * LANES:(t + 1) * LANES] >= cand, 1, 0)
            return cnt
        cnt = lax.fori_loop(0, n_chunks, body, jnp.zeros((tq, LANES), I32))
        return jnp.sum(cnt, axis=1, keepdims=True)

    def bit_step(i, lo):
        cand = lo + lax.shift_left(jnp.int32(1), 31 - i)
        return jnp.where(count_ge(cand) >= top_k, cand, lo)

    lo = lax.fori_loop(0, 32, bit_step, jnp.full((tq, LANES), INT_MIN, I32))
    thr = jnp.maximum(lo, INT_MIN + 1)

    o_ref[0] = jnp.full((tq, s_pad), NEG_INF, o_ref.dtype)

    def write_chunk(c, _):
        off = pl.multiple_of(c * ck, ck)
        kc = key_ref[:, pl.ds(off, ck)]
        o_ref[0, :, pl.ds(off, ck)] = jnp.where(kc >= thr[:, :1], 0.0, NEG_INF).astype(o_ref.dtype)
        return 0

    lax.fori_loop(0, n_chunks, write_chunk, 0)


def _index_call(iq, iw, ik, q_pos0, s_valid, tq, ck):
    bsz, t_q, _ = iq.shape
    s_pad = ik.shape[1]
    top_k = min(IDX_TOPK, s_valid // 4)
    return pl.pallas_call(
        functools.partial(_index_kernel, q_pos0=q_pos0, s_valid=s_valid, tq=tq, ck=ck, top_k=top_k),
        grid=(bsz, t_q // tq),
        in_specs=[
            pl.BlockSpec((1, tq, IDX_WIDTH), lambda b, i: (b, i, 0)),
            pl.BlockSpec((1, tq, LANES), lambda b, i: (b, i, 0)),
            pl.BlockSpec((1, s_pad, IDX_DIM), lambda b, i: (b, 0, 0)),
        ],
        out_specs=pl.BlockSpec((1, tq, s_pad), lambda b, i: (b, i, 0)),
        out_shape=jax.ShapeDtypeStruct((bsz, t_q, s_pad), BF16),
        scratch_shapes=[pltpu.VMEM((tq, s_pad), I32), pltpu.VMEM((IDX_HEADS, tq, LANES), F32)],
        compiler_params=_cparams(("arbitrary", "arbitrary")),
        name="indexer_topk_mask",
    )(iq, iw, ik)


def _diff_kernel(q_ref, k_ref, v_ref, dl_ref, g_ref, o_ref, m_ref, acc_ref, *, q_pos0, s_valid, tq, tk, lam_init):
    iq, ik = pl.program_id(2), pl.program_id(3)
    nk = pl.num_programs(3)
    c = float(C_QK_DIM ** -0.5) * LOG2E

    @pl.when(ik == 0)
    def _():
        m_ref[...] = jnp.full_like(m_ref, NEG_INF)
        acc_ref[...] = jnp.zeros_like(acc_ref)

    last = _last_block(iq, "chunk", q_pos0, tq, tk, s_valid)
    first_masked = (q_pos0 + iq * tq) // tk

    def step(masked):
        if masked:
            q_pos = q_pos0 + iq * tq + lax.broadcasted_iota(I32, (tq, 1), 0)
            k_pos = ik * tk + lax.broadcasted_iota(I32, (tq, tk), 1)
            visible = k_pos < _chunk_limit(q_pos, s_valid)
        for hh in range(HEADS_PER_STEP):
            sl = slice(hh * LANES, (hh + 1) * LANES)
            q = q_ref[0, :, sl]
            k = k_ref[0, :, sl]
            v_ext = _with_ones(v_ref[0, :, sl])
            lane = lax.broadcasted_iota(I32, q.shape, 1)
            for mm, sel in enumerate((lane < C_QK_DIM, lane >= C_QK_DIM)):
                s = _dot_nt(jnp.where(sel, q, jnp.zeros_like(q)), k) * c
                if masked:
                    s = jnp.where(visible, s, NEG_INF)
                _online_softmax_step(s, v_ext, m_ref, acc_ref, 2 * hh + mm)

    @pl.when(ik < jnp.minimum(first_masked, last + 1))
    def _():
        step(False)

    @pl.when((ik >= first_masked) & (ik <= last))
    def _():
        step(True)

    @pl.when(ik == nk - 1)
    def _():
        dl = dl_ref[...]
        lam = (jnp.exp(jnp.sum(dl[0:1] * dl[1:2], axis=1, keepdims=True))
               - jnp.exp(jnp.sum(dl[2:3] * dl[3:4], axis=1, keepdims=True)) + lam_init)
        outs = []
        for hh in range(HEADS_PER_STEP):
            o = _softmax_result(acc_ref, 2 * hh) - lam * _softmax_result(acc_ref, 2 * hh + 1)
            o = o * lax.rsqrt(jnp.mean(o * o, axis=1, keepdims=True) + SUBLN_EPS)
            outs.append(o * g_ref[...] * (1.0 - lam_init))
        o_ref[0] = jnp.concatenate(outs, axis=1).astype(o_ref.dtype)


def _diff_call(q, k, v, dl, g, q_pos0, s_valid, tq, tk, lam_init):
    bsz, t_q, _ = q.shape
    s_pad = k.shape[1]
    hps = HEADS_PER_STEP
    assert C_HEADS % hps == 0
    last = functools.partial(_last_block, mode="chunk", q_pos0=q_pos0, tq=tq, tk=tk, s_valid=s_valid)

    def kv_map(b, hp, iq, ik):
        return (b, jnp.minimum(ik, last(iq)), hp)

    return pl.pallas_call(
        functools.partial(_diff_kernel, q_pos0=q_pos0, s_valid=s_valid, tq=tq, tk=tk, lam_init=lam_init),
        grid=(bsz, C_HEADS // hps, t_q // tq, s_pad // tk),
        in_specs=[
            pl.BlockSpec((1, tq, hps * LANES), lambda b, hp, iq, ik: (b, iq, hp)),
            pl.BlockSpec((1, tk, hps * LANES), kv_map),
            pl.BlockSpec((1, tk, hps * LANES), kv_map),
            pl.BlockSpec((4, C_QK_DIM), lambda b, hp, iq, ik: (0, 0)),
            pl.BlockSpec((1, LANES), lambda b, hp, iq, ik: (0, 0)),
        ],
        out_specs=pl.BlockSpec((1, tq, hps * LANES), lambda b, hp, iq, ik: (b, iq, hp)),
        out_shape=jax.ShapeDtypeStruct((bsz, t_q, C_WIDTH), BF16),
        scratch_shapes=[pltpu.VMEM((2 * hps, tq, LANES), F32), pltpu.VMEM((2 * hps, tq, 2 * LANES), F32)],
        compiler_params=_cparams(("arbitrary",) * 4),
        name="diff_attention",
    )(q, k, v, dl, g)


def _out_kernel(x_ref, oa_ref, ob_ref, oc_ref, w_ref, gt_ref, sc_ref, sh_ref, g_ref, wr_ref, br_ref,
                x1_ref, h_ref, ti_ref, tw_ref):
    attn = (_dot(oa_ref[...], w_ref[0:A_WIDTH, :])
            + _dot(ob_ref[...], w_ref[A_WIDTH:A_WIDTH + B_WIDTH, :])
            + _dot(oc_ref[...], w_ref[A_WIDTH + B_WIDTH:, :]))
    x1 = x_ref[...] + gt_ref[0] * attn
    x1_ref[...] = x1
    h = _rms_mod(x1, g_ref[...], sc_ref[0], sh_ref[0])
    h_ref[...] = h
    logits = _dot(h, wr_ref[...], precision=lax.Precision.HIGHEST) + br_ref[...]
    lane = lax.broadcasted_iota(I32, logits.shape, 1)
    vals, idxs = [], []
    for _ in range(TOP_K):
        mx = jnp.max(logits, axis=1, keepdims=True)
        ix = jnp.min(jnp.where(logits == mx, lane, LANES), axis=1, keepdims=True)
        vals.append(mx)
        idxs.append(ix)
        logits = jnp.where(lane == ix, -jnp.inf, logits)
    es = [jnp.exp(v - vals[0]) for v in vals]
    den = es[0] + es[1] + es[2] + es[3]
    ti = jnp.zeros(lane.shape, I32)
    tw = jnp.zeros(lane.shape, F32)
    for kk in range(TOP_K):
        ti = jnp.where(lane == kk, idxs[kk], ti)
        tw = jnp.where(lane == kk, es[kk] / den, tw)
    ti_ref[...] = ti
    tw_ref[...] = tw


def _out_call(x2d, oa, ob, oc, w_out, gt, sc, sh, g, w_router, b_router, tm, tiles_per_mod):
    n, d = x2d.shape
    r = gt.shape[1]
    row = lambda w: pl.BlockSpec((tm, w), lambda i: (i, 0))
    mod_spec = pl.BlockSpec((1, r, d), lambda i: (i // tiles_per_mod, 0, 0))
    full = lambda a: pl.BlockSpec(a.shape, lambda i: (0,) * a.ndim)
    return pl.pallas_call(
        _out_kernel,
        grid=(n // tm,),
        in_specs=[row(d), row(A_WIDTH), row(B_WIDTH), row(C_WIDTH), full(w_out), mod_spec, mod_spec, mod_spec,
                  full(g), full(w_router), full(b_router)],
        out_specs=[row(d), row(d), row(LANES), row(LANES)],
        out_shape=[jax.ShapeDtypeStruct((n, d), F32), jax.ShapeDtypeStruct((n, d), F32),
                   jax.ShapeDtypeStruct((n, LANES), I32), jax.ShapeDtypeStruct((n, LANES), F32)],
        compiler_params=_cparams(("arbitrary",)),
        name="out_proj_router",
    )(x2d, oa, ob, oc, w_out, gt, sc, sh, g, w_router, b_router)


def _expert_kernel(te_ref, nv_ref, nt_ref, gidx_hbm, sidx_hbm, h_hbm, wg_ref, wu_ref, bg_ref, bu_ref, wd_ref, bd_ref,
                   z_hbm, gidx, sidx, xbuf, xb_ref, acc_ref, ybuf, sem_gi, sem_si, sem_g, sem_s, *, n_tiles):
    t, j = pl.program_id(0), pl.program_id(1)
    nf = pl.num_programs(1)
    used = t < nt_ref[0]
    slot = t % 2
    other = 1 - slot

    def gidx_copy(tt, sl):
        return pltpu.make_async_copy(gidx_hbm.at[tt], gidx.at[sl], sem_gi.at[sl])

    def sidx_copy(tt, sl):
        return pltpu.make_async_copy(sidx_hbm.at[tt], sidx.at[sl], sem_si.at[sl])

    def gather_row(sl, r, token):
        return pltpu.make_async_copy(h_hbm.at[pl.ds(token, 1)], xbuf.at[sl, pl.ds(r, 1)], sem_g.at[sl])

    def scatter_row(sl, r, dest):
        return pltpu.make_async_copy(ybuf.at[sl, pl.ds(r, 1)], z_hbm.at[pl.ds(dest, 1)], sem_s.at[sl])

    def for_valid_rows(tt, body):
        nv = nv_ref[tt]
        n_groups = lax.shift_right_logical(nv, ROW_UNROLL.bit_length() - 1)

        def group(g, _):
            for u in range(ROW_UNROLL):
                body(g * ROW_UNROLL + u)
            return 0

        def single(r, _):
            body(r)
            return 0

        lax.fori_loop(0, n_groups, group, 0)
        lax.fori_loop(n_groups * ROW_UNROLL, nv, single, 0)

    def wait_valid_rows(tt, block_copy):
        nv = nv_ref[tt]
        rows = xbuf.shape[1]
        while rows >= 1:
            @pl.when((nv & rows) != 0)
            def _(rows=rows):
                block_copy(rows).wait()
            rows //= 2

    def start_gather(tt, sl):
        for_valid_rows(tt, lambda r: gather_row(sl, r, gidx[sl, r]).start())

    def wait_gather(tt, sl):
        wait_valid_rows(tt, lambda rows: pltpu.make_async_copy(
            h_hbm.at[pl.ds(0, rows)], xbuf.at[sl, pl.ds(0, rows)], sem_g.at[sl]))

    def start_scatter(tt, sl):
        for_valid_rows(tt, lambda r: scatter_row(sl, r, sidx[sl, r]).start(priority=1))

    def wait_scatter(tt, sl):
        wait_valid_rows(tt, lambda rows: pltpu.make_async_copy(
            ybuf.at[sl, pl.ds(0, rows)], z_hbm.at[pl.ds(0, rows)], sem_s.at[sl]))

    @pl.when((t == 0) & (j == 0))
    def _():
        xbuf[...] = jnp.zeros_like(xbuf)
        gidx_copy(0, 0).start()
        gidx_copy(0, 0).wait()
        start_gather(0, 0)
        if n_tiles > 1:
            gidx_copy(1, 1).start()

    @pl.when(j == 0)
    def _():
        wait_gather(t, slot)

        @pl.when(t + 1 < n_tiles)
        def _():
            gidx_copy(t + 1, other).wait()
            start_gather(t + 1, other)

        @pl.when(t + 2 < n_tiles)
        def _():
            gidx_copy(t + 2, slot).start()

        sidx_copy(t, slot).start()

    @pl.when(used & (j == 0))
    def _():
        xb_ref[...] = xbuf[slot].astype(BF16)
        acc_ref[...] = jnp.zeros_like(acc_ref)

    @pl.when(used)
    def _():
        x = xb_ref[...]
        gate = jnp.minimum(_dot(x, wg_ref[0]) + bg_ref[0], SWIGLU_LIMIT)
        up = jnp.clip(_dot(x, wu_ref[0]) + bu_ref[0], -SWIGLU_LIMIT, SWIGLU_LIMIT)
        act = (up + 1.0) * (gate * jax.nn.sigmoid(gate * SWIGLU_ALPHA))
        acc_ref[...] += _dot(act.astype(BF16), wd_ref[0])

    @pl.when(j == nf - 1)
    def _():
        @pl.when(t >= 2)
        def _():
            wait_scatter(t - 2, slot)

        @pl.when(used)
        def _():
            ybuf[slot] = acc_ref[...] + bd_ref[0]

        sidx_copy(t, slot).wait()
        start_scatter(t, slot)

        @pl.when(t == n_tiles - 1)
        def _():
            @pl.when(t >= 1)
            def _():
                wait_scatter(t - 1, other)

            wait_scatter(t, slot)


def _expert_call(h_all, src_row, dest_row, tile_expert, n_valid, n_tiles_used, w_gu, b_gu, w_d, b_d, tm, tf):
    n, d = h_all.shape
    n_exp, _, f2 = w_gu.shape
    n_tiles = src_row.shape[0] // tm
    f = f2 // 2
    nf = f // tf
    any_spec = pl.BlockSpec(memory_space=pl.ANY)
    grid_spec = pltpu.PrefetchScalarGridSpec(
        num_scalar_prefetch=3,
        grid=(n_tiles, nf),
        in_specs=[
            any_spec, any_spec, any_spec,
            pl.BlockSpec((1, d, tf), lambda t, j, te, nv, nt: (te[t], 0, j)),
            pl.BlockSpec((1, d, tf), lambda t, j, te, nv, nt: (te[t], 0, nf + j)),
            pl.BlockSpec((1, 1, tf), lambda t, j, te, nv, nt: (te[t], 0, j)),
            pl.BlockSpec((1, 1, tf), lambda t, j, te, nv, nt: (te[t], 0, nf + j)),
            pl.BlockSpec((1, tf, d), lambda t, j, te, nv, nt: (te[t], j, 0)),
            pl.BlockSpec((1, 1, d), lambda t, j, te, nv, nt: (te[t], 0, 0)),
        ],
        out_specs=any_spec,
        scratch_shapes=[
            pltpu.SMEM((2, tm), I32), pltpu.SMEM((2, tm), I32),
            pltpu.VMEM((2, tm, d), F32), pltpu.VMEM((tm, d), BF16), pltpu.VMEM((tm, d), F32),
            pltpu.VMEM((2, tm, d), F32),
            pltpu.SemaphoreType.DMA((2,)), pltpu.SemaphoreType.DMA((2,)),
            pltpu.SemaphoreType.DMA((2,)), pltpu.SemaphoreType.DMA((2,)),
        ],
    )
    return pl.pallas_call(
        functools.partial(_expert_kernel, n_tiles=n_tiles),
        grid_spec=grid_spec,
        out_shape=jax.ShapeDtypeStruct((TOP_K * n, d), F32),
        compiler_params=_cparams(("arbitrary", "arbitrary")),
        name="moe_experts",
    )(tile_expert, n_valid, n_tiles_used, src_row.reshape(n_tiles, tm), dest_row.reshape(n_tiles, tm), h_all,
      w_gu, w_gu, b_gu.reshape(n_exp, 1, f2), b_gu.reshape(n_exp, 1, f2), w_d, b_d.reshape(n_exp, 1, d))


def _combine_kernel(x_ref, z0, z1, z2, z3, tw_ref, gt_ref, gf_ref, o_ref, *, final_norm):
    tw = tw_ref[...]
    moe = jnp.zeros(x_ref.shape, F32)
    for kk, z in enumerate((z0, z1, z2, z3)):
        moe = moe + tw[:, kk:kk + 1] * z[0]
    x2 = x_ref[...] + gt_ref[0] * moe
    if final_norm:
        x2 = x2 * lax.rsqrt(jnp.mean(x2 * x2, axis=-1, keepdims=True) + RMS_EPS) * gf_ref[...]
    o_ref[...] = x2


def _combine_call(z, tok0, x1, top_w, gt, tiles_per_mod, g_final, tn, final_norm):
    n, d = x1.shape
    r = gt.shape[1]
    blk0 = tok0 // tn
    z_specs = [pl.BlockSpec((1, tn, d), lambda i, kk=kk: (kk, blk0 + i, 0)) for kk in range(TOP_K)]
    return pl.pallas_call(
        functools.partial(_combine_kernel, final_norm=final_norm),
        grid=(n // tn,),
        in_specs=[pl.BlockSpec((tn, d), lambda i: (i, 0))] + z_specs + [
            pl.BlockSpec((tn, LANES), lambda i: (i, 0)),
            pl.BlockSpec((1, r, d), lambda i: (i // tiles_per_mod, 0, 0)),
            pl.BlockSpec((1, d), lambda i: (0, 0)),
        ],
        out_specs=pl.BlockSpec((tn, d), lambda i: (i, 0)),
        out_shape=jax.ShapeDtypeStruct((n, d), F32),
        compiler_params=_cparams(("arbitrary",)),
        name="moe_combine",
    )(x1, z, z, z, z, top_w, gt, g_final)


def _routing_plan(top_i, n_exp, tm):
    n = top_i.shape[0]
    m = n * TOP_K
    flat = top_i.reshape(-1)
    order = jnp.argsort(flat, stable=True).astype(I32)
    sorted_e = flat[order]
    experts = jnp.arange(n_exp, dtype=I32)
    end = jnp.sum(sorted_e[None, :] <= experts[:, None], axis=1).astype(I32)
    counts = jnp.diff(end, prepend=0)
    start = end - counts
    padded = (counts + tm - 1) // tm * tm
    pend = jnp.cumsum(padded)
    pstart = pend - padded
    p = (m + n_exp * (tm - 1)) // tm * tm
    rows = jnp.arange(p, dtype=I32)
    e_row = jnp.minimum(jnp.sum(rows[:, None] >= pend[None, :], axis=1), n_exp - 1).astype(I32)
    rank = rows - pstart[e_row]
    valid = rank < counts[e_row]
    pair = order[jnp.clip(start[e_row] + rank, 0, m - 1)]
    src_row = jnp.where(valid, pair // TOP_K, 0)
    dest_row = jnp.where(valid, (pair % TOP_K) * n + pair // TOP_K, 0)
    tile_expert = e_row[::tm]
    n_valid = jnp.clip(counts[tile_expert] - rank[::tm], 0, tm).astype(I32)
    n_tiles_used = (pend[-1] // tm).astype(I32).reshape(1)
    return src_row, dest_row, tile_expert, n_valid, n_tiles_used


def _pad_rows(a, axis, size):
    pad = size - a.shape[axis]
    if pad == 0:
        return a
    widths = [(0, 0)] * a.ndim
    widths[axis] = (0, pad)
    return jnp.pad(a, widths)


def _mixers(pr, new_logf, past, q_pos0, dl, g_diff, lam_init, tq, tk, tq_idx, ck_idx):
    bsz, t = pr["fq_b"].shape[:2]
    keys = {}
    if past is None:
        for nm in ("fk", "fv", "dk", "dv", "ik", "ck", "cv"):
            keys[nm] = pr[nm + "_b"]
        logf_all = new_logf
        s_valid = t
    else:
        s_valid = past["fk"].shape[1] + t
        s_pad = -(-s_valid // ck_idx) * ck_idx
        tk = s_pad
        for nm in ("fk", "fv", "dk", "dv", "ik", "ck", "cv"):
            cache = past[nm].reshape(bsz, past[nm].shape[1], -1).astype(BF16)
            keys[nm] = _pad_rows(jnp.concatenate([cache, pr[nm + "_b"]], axis=1), 1, s_pad)
        logf_all = jnp.concatenate([past["logf"], new_logf], axis=1)
    s_pad = keys["fk"].shape[1]

    chan = jnp.transpose(logf_all, (1, 0, 2)).reshape(logf_all.shape[1], bsz * A_HEADS)
    cum = _cumsum_call(_pad_rows(_pad_rows(chan, 1, LANES), 0, s_pad))
    o_a = _flash_call("fox", pr["fq_b"], keys["fk"], keys["fv"], (cum, cum.T), A_HEADS, q_pos0, s_valid, tq, tk)

    bias = _index_call(pr["iq_b"], pr["small"], keys["ik"], q_pos0, s_valid, tq_idx, ck_idx)
    o_b = _flash_call("dsa", pr["dq_b"], keys["dk"], keys["dv"], (bias,), B_HEADS, q_pos0, s_valid, tq, tk)

    o_c = _diff_call(pr["cq_b"], keys["ck"], keys["cv"], dl, g_diff, q_pos0, s_valid, tq, tk, lam_init)
    return o_a, o_b, o_c


def kernel(x_prompt, x_sample, cache_fox_k, cache_fox_v, cache_fox_logf, cache_dsa_k, cache_dsa_v,
           cache_dsa_idx_k, cache_diff_k, cache_diff_v, c_prompt, c_sample, w_ada, b_ada, g_attn, g_ffn,
           w_in, b_forget, diff_lambda, g_diff, w_out, w_router, b_router, w_gate_up, b_gate_up,
           w_down, b_down, g_final):
    depth = w_ada.shape[0]
    bp, tp, d = x_prompt.shape
    bs, ts, _ = x_sample.shape
    past_len = cache_fox_k.shape[2]
    n_exp = w_router.shape[-1]
    np_, ns = bp * tp, bs * ts

    tm_p = min(512, tp)
    tm_o = min(512, tp)
    tq = min(512, tp)
    tk = min(1024, tp)
    tq_idx = min(128, tp)
    ck_idx = min(512, tp)
    tm_e = 512
    tn_c = min(256, ns)

    m_rows = -(-(bp + bs) // 16) * 16
    c_all = _pad_rows(jnp.concatenate([c_prompt, c_sample], axis=0), 0, m_rows)
    mod = _ada_call(c_all, w_ada, b_ada)

    tabs_p = _rope_tables(jnp.arange(tp, dtype=I32))
    tabs_s = [jnp.tile(tb, (bs, 1)) for tb in _rope_tables(jnp.arange(past_len, past_len + ts, dtype=I32))]

    offs = {}
    acc = 0
    for nm, wdt in (("fq", A_WIDTH), ("fk", A_WIDTH), ("fv", A_WIDTH), ("ff", A_HEADS), ("dq", B_WIDTH),
                    ("dk", B_WIDTH), ("dv", B_WIDTH), ("iq", IDX_WIDTH), ("ik", IDX_DIM), ("iw", IDX_HEADS),
                    ("cq", C_WIDTH), ("ck", C_WIDTH), ("cv", C_WIDTH)):
        offs[nm] = (acc, acc + wdt)
        acc += wdt
    col_order = ("fq", "fk", "fv", "dq", "dk", "dv", "iq", "cq", "ck", "cv", "ik", "ff", "iw")

    xp = x_prompt.reshape(np_, d)
    xs = x_sample.reshape(ns, d)
    rows_p, rows_s = [], []
    for l in range(depth):
        lam_init = 0.8 - 0.6 * math.exp(-0.3 * l)
        w_cat = jnp.concatenate([w_in[l][:, offs[nm][0]:offs[nm][1]] for nm in col_order], axis=1)
        w_cat = _pad_rows(w_cat, 1, _PROJ_TILES * PROJ_TN).astype(BF16)
        b_small = _pad_rows(b_forget[l].reshape(1, A_HEADS), 1, LANES)
        w_out_b = w_out[l].astype(BF16)
        w_r = _pad_rows(w_router[l], 1, LANES)
        b_r = jnp.concatenate([b_router[l], jnp.full((LANES - n_exp,), NEG_INF, F32)]).reshape(1, LANES)
        w_gu_b = w_gate_up[l].astype(BF16)
        w_d_b = w_down[l].astype(BF16)

        mods_p = [m.reshape(bp, 1, d) for m in jnp.split(mod[l, :bp], 6, axis=-1)]
        mods_s = [jnp.broadcast_to(m[:, None, :], (bs, ts, d)).reshape(ns, d)
                  for m in jnp.split(mod[l, bp:bp + bs], 6, axis=-1)]

        def mod_view(m, t, tile):
            if m.ndim == 3:
                return m, t // tile
            return m.reshape(m.shape[0] // tile, tile, d), 1

        groups = []
        for (x2d, mods, tabs, bsz, t, tm, past, q_pos0, tqg, tqi) in (
                (xp, mods_p, tabs_p, bp, tp, tm_p, None, 0, tq, tq_idx),
                (xs, mods_s, tabs_s, bs, ts, ns,
                 dict(fk=cache_fox_k[l], fv=cache_fox_v[l], logf=cache_fox_logf[l], dk=cache_dsa_k[l],
                      dv=cache_dsa_v[l], ik=cache_dsa_idx_k[l], ck=cache_diff_k[l], cv=cache_diff_v[l]),
                 past_len, ts, ts)):
            sh1, sc1, gt1, sh2, sc2, gt2 = mods
            (sc1v, tpm), (sh1v, _) = mod_view(sc1, t, tm), mod_view(sh1, t, tm)
            pr = _proj_call(x2d, sc1v, sh1v, g_attn[l].reshape(1, d), w_cat, tabs, b_small, tm, tpm)
            pr3 = {nm: a.reshape(bsz, t, a.shape[-1]) for nm, a in pr.items()}
            new_logf = pr3["small"][:, :, :A_HEADS]
            o_a, o_b, o_c = _mixers(pr3, new_logf, past, q_pos0, diff_lambda[l], g_diff[l].reshape(1, LANES),
                                    lam_init, tqg, tk, tqi, ck_idx)
            tmo = min(tm_o, bsz * t)
            (gt1v, tpm_o), (sc2v, _), (sh2v, _) = (mod_view(m, t, tmo) for m in (gt1, sc2, sh2))
            x1, h2, top_i, top_w = _out_call(
                x2d, o_a.reshape(bsz * t, -1), o_b.reshape(bsz * t, -1), o_c.reshape(bsz * t, -1), w_out_b,
                gt1v, sc2v, sh2v, g_ffn[l].reshape(1, d), w_r, b_r, tmo, tpm_o)
            rows = (pr3["fk_f"].reshape(bsz, t, A_HEADS, HEAD_DIM), pr3["fv_f"].reshape(bsz, t, A_HEADS, HEAD_DIM),
                    new_logf, pr3["dk_f"].reshape(bsz, t, B_HEADS, HEAD_DIM),
                    pr3["dv_f"].reshape(bsz, t, B_HEADS, HEAD_DIM), pr3["ik_f"],
                    pr3["ck_f"].reshape(bsz, t, C_HEADS, 2, C_QK_DIM), pr3["cv_f"].reshape(bsz, t, C_HEADS, 2 * C_QK_DIM))
            groups.append((x1, h2, top_i, top_w, gt2, rows))

        (x1p, h2p, tip, twp, gt2p, rp), (x1s, h2s, tis, tws, gt2s, rs) = groups
        rows_p.append(rp)
        rows_s.append(rs)

        h_all = jnp.concatenate([h2p, h2s], axis=0)
        ti_all = jnp.concatenate([tip[:, :TOP_K], tis[:, :TOP_K]], axis=0)
        src_row, dest_row, tile_expert, n_valid, n_tiles_used = _routing_plan(ti_all, n_exp, tm_e)
        z = _expert_call(h_all, src_row, dest_row, tile_expert, n_valid, n_tiles_used, w_gu_b, b_gate_up[l],
                         w_d_b, b_down[l], tm_e, min(512, w_d_b.shape[1]))
        z = z.reshape(TOP_K, np_ + ns, d)
        final = l == depth - 1
        gf = g_final.reshape(1, d)
        xp = _combine_call(z, 0, x1p, twp, *mod_view(gt2p, tp, tn_c), gf, tn_c, final)
        xs = _combine_call(z, np_, x1s, tws, *mod_view(gt2s, ts, tn_c), gf, tn_c, final)

    def stack(rows, i):
        return jnp.stack([r[i] for r in rows], axis=0)

    return ((xp.reshape(bp, tp, d), xs.reshape(bs, ts, d))
            + tuple(stack(rows_p, i) for i in range(8)) + tuple(stack(rows_s, i) for i in range(8)))
```
